```python
import math
import jax, jax.numpy as jnp
from jax import lax
import numpy as np

D_MODEL = 1024
BATCH = 2
SEQ = 16384
DEPTH = 2
DEC_BATCH = 8
DEC_SEQ = 8192
PAST_LEN = 128

CONV_CH = 512
CONV_W = 3
RET_HEADS = 4
RET_DIM = 128
RET_WIDTH = RET_HEADS * RET_DIM
RET_CHUNK = 128
DIFF_HEADS = 8
DIFF_DIM = 64
Q_BLOCK = 128
D_FF = ((8 * D_MODEL + 3 * 256 - 1) // (3 * 256)) * 256
ROPE_THETA = 10000.0
NORM_EPS = 1e-6
GN_EPS = 1e-5
SUBLN_EPS = 1e-5
IN_PROJ_COLS = 3 * CONV_CH + 4 * RET_WIDTH
IN_SPLITS = [CONV_CH, 2 * CONV_CH, 3 * CONV_CH, 3 * CONV_CH + RET_WIDTH,
             3 * CONV_CH + 2 * RET_WIDTH, 3 * CONV_CH + 3 * RET_WIDTH]
DIFF_QK = 2 * DIFF_HEADS * DIFF_DIM
DIFF_V = DIFF_HEADS * 2 * DIFF_DIM
N_EVEN = (DEPTH + 1) // 2
N_ODD = DEPTH // 2

kernel_name = "hybrid_conv_retention_diffattn_encoder"

F32 = jnp.float32


def _rmsnorm(x, w, eps=NORM_EPS):
    xf = x.astype(F32)
    y = xf * lax.rsqrt(jnp.mean(xf * xf, axis=-1, keepdims=True) + eps)
    return y.astype(x.dtype) * w


def _rope(x):
    s, d = x.shape[-2], x.shape[-1]
    inv = ROPE_THETA ** (-jnp.arange(0, d, 2, dtype=F32) / d)
    ang = jnp.arange(s, dtype=F32)[:, None] * inv[None, :]
    cos = jnp.cos(ang).astype(x.dtype)
    sin = jnp.sin(ang).astype(x.dtype)
    x1, x2 = x[..., : d // 2], x[..., d // 2:]
    return jnp.concatenate([x1 * cos - x2 * sin, x1 * sin + x2 * cos], axis=-1)


def _retention_one_dir(q, k, v, log_gamma, strict):
    b, h, s, d = q.shape
    c = RET_CHUNK
    n = s // c
    dt = q.dtype
    qc = q.reshape(b, h, n, c, d)
    kc = k.reshape(b, h, n, c, d)
    vc = v.reshape(b, h, n, c, d)
    idx = jnp.arange(c, dtype=F32)
    rel = idx[:, None] - idx[None, :]
    mask = rel > 0 if strict else rel >= 0
    lg = log_gamma.astype(F32)
    decay_intra = jnp.where(mask[None], jnp.exp(lg[:, None, None] * jnp.maximum(rel, 0.0)[None]), 0.0).astype(dt)
    scores = jnp.einsum('bhnid,bhnjd->bhnij', qc, kc) * decay_intra[None, :, None]
    o_intra = jnp.einsum('bhnij,bhnjd->bhnid', scores, vc)
    k_decay = jnp.exp(lg[:, None] * (c - 1 - idx)[None]).astype(dt)
    kv = jnp.einsum('bhnjd,hj,bhnje->bhnde', kc, k_decay, vc)
    chunk_decay = jnp.exp(lg * c).astype(dt)[None, :, None, None]

    def step(state, kv_t):
        return state * chunk_decay + kv_t, state

    _, s_prev = lax.scan(step, jnp.zeros((b, h, d, d), dt), jnp.moveaxis(kv, 2, 0))
    s_prev = jnp.moveaxis(s_prev, 0, 2)
    q_decay = jnp.exp(lg[:, None] * (idx + 1.0)[None]).astype(dt)
    o_cross = jnp.einsum('bhnid,hi,bhnde->bhnie', qc, q_decay, s_prev)
    return (o_intra + o_cross).reshape(b, h, s, d)


def _hybrid_conv_retention(xn, w_in, conv_w, decay_fwd, decay_bwd, gn_w, w_out):
    b, s, _ = xn.shape
    proj = xn @ w_in
    a_b, a_c, a_h, r_q, r_k, r_v, r_g = jnp.split(proj, IN_SPLITS, axis=-1)
    u = a_c * a_h
    up = jnp.pad(u, ((0, 0), (1, 1), (0, 0)))
    conv = conv_w[0] * up[:, :-2] + conv_w[1] * up[:, 1:-1] + conv_w[2] * up[:, 2:]
    y_a = a_b * conv
    def heads(t):
        return t.reshape(b, s, RET_HEADS, RET_DIM).transpose(0, 2, 1, 3)
    q = _rope(heads(r_q))
    k = _rope(heads(r_k)) * (RET_DIM ** -0.5)
    v = heads(r_v)
    lg_f = -jnp.exp(decay_fwd.astype(F32))
    lg_b = -jnp.exp(decay_bwd.astype(F32))
    o_f = _retention_one_dir(q, k, v, lg_f, strict=False)
    o_b = jnp.flip(_retention_one_dir(jnp.flip(q, 2), jnp.flip(k, 2), jnp.flip(v, 2), lg_b, strict=True), 2)
    o = (o_f + o_b).astype(F32)
    mu = jnp.mean(o, axis=-1, keepdims=True)
    var = jnp.mean(jnp.square(o - mu), axis=-1, keepdims=True)
    o = ((o - mu) * lax.rsqrt(var + GN_EPS)).astype(xn.dtype)
    o = o.transpose(0, 2, 1, 3).reshape(b, s, RET_WIDTH) * gn_w
    y_b = jax.nn.silu(r_g) * o
    return jnp.concatenate([y_a, y_b], axis=-1) @ w_out


def _diff_attention(xn, w_qkv, lq1, lk1, lq2, lk2, subln, w_out, lambda_init):
    b, s, _ = xn.shape
    qkv = xn @ w_qkv
    q, k, v = jnp.split(qkv, [DIFF_QK, 2 * DIFF_QK], axis=-1)
    q = q.reshape(b, s, 2 * DIFF_HEADS, DIFF_DIM).transpose(0, 2, 1, 3)
    k = k.reshape(b, s, 2 * DIFF_HEADS, DIFF_DIM).transpose(0, 2, 1, 3)
    v = v.reshape(b, s, DIFF_HEADS, 2 * DIFF_DIM).transpose(0, 2, 1, 3)
    q = _rope(q) * (DIFF_DIM ** -0.5)
    k = _rope(k)
    lam = (jnp.exp(jnp.sum(lq1.astype(F32) * lk1.astype(F32)))
           - jnp.exp(jnp.sum(lq2.astype(F32) * lk2.astype(F32))) + lambda_init)
    nb = s // Q_BLOCK
    qb = q.reshape(b, 2 * DIFF_HEADS, nb, Q_BLOCK, DIFF_DIM).transpose(2, 0, 1, 3, 4)

    def block(qblk):
        sc = jnp.einsum('bhqd,bhkd->bhqk', qblk, k).astype(F32)
        p = jax.nn.softmax(sc, axis=-1).reshape(b, DIFF_HEADS, 2, Q_BLOCK, s)
        a = p[:, :, 0] - lam * p[:, :, 1]
        return jnp.einsum('bhqk,bhke->bhqe', a.astype(v.dtype), v)

    o = lax.map(block, qb)
    o = o.transpose(1, 2, 0, 3, 4).reshape(b, DIFF_HEADS, s, 2 * DIFF_DIM)
    o = _rmsnorm(o, subln, SUBLN_EPS) * (1.0 - lambda_init)
    o = o.transpose(0, 2, 1, 3).reshape(b, s, DIFF_V)
    return o @ w_out


def _swiglu(xn, w_gate, w_up, w_down):
    return (jax.nn.silu(xn @ w_gate) * (xn @ w_up)) @ w_down


def _trunk(x, norm_mix, norm_ffn, norm_final, hyb_w_in, hyb_conv_w, hyb_decay_fwd, hyb_decay_bwd,
           hyb_gn, hyb_w_out, diff_w_qkv, diff_lq1, diff_lk1, diff_lq2, diff_lk2, diff_subln,
           diff_w_out, ffn_w_gate, ffn_w_up, ffn_w_down):
    for layer in range(DEPTH):
        xn = _rmsnorm(x, norm_mix[layer])
        if layer % 2 == 0:
            e = layer // 2
            x = x + _hybrid_conv_retention(xn, hyb_w_in[e], hyb_conv_w[e], hyb_decay_fwd[e],
                                           hyb_decay_bwd[e], hyb_gn[e], hyb_w_out[e])
        else:
            o = layer // 2
            lambda_init = 0.8 - 0.6 * math.exp(-0.3 * layer)
            x = x + _diff_attention(xn, diff_w_qkv[o], diff_lq1[o], diff_lk1[o], diff_lq2[o],
                                    diff_lk2[o], diff_subln[o], diff_w_out[o], lambda_init)
        x = x + _swiglu(_rmsnorm(x, norm_ffn[layer]), ffn_w_gate[layer], ffn_w_up[layer], ffn_w_down[layer])
    return _rmsnorm(x, norm_final)


def setup_inputs(seed: int = 0) -> dict:
    key = jax.random.key(seed)
    ks = jax.random.split(key, 24)

    def nrm(k, shape, fan_in):
        return jax.random.normal(k, shape, F32) * (fan_in ** -0.5)

    def gain(k, shape):
        return 1.0 + 0.02 * jax.random.normal(k, shape, F32)

    base_decay = jnp.log(-jnp.log1p(-(2.0 ** (-5.0 - jnp.arange(RET_HEADS, dtype=F32)))))
    return {
        "x_prompt": jax.random.normal(ks[0], (BATCH, SEQ, D_MODEL), F32),
        "x_sample": jax.random.normal(ks[1], (DEC_BATCH, DEC_SEQ, D_MODEL), F32),
        "norm_mix": gain(ks[2], (DEPTH, D_MODEL)),
        "norm_ffn": gain(ks[3], (DEPTH, D_MODEL)),
        "norm_final": gain(ks[4], (D_MODEL,)),
        "hyb_w_in": nrm(ks[5], (N_EVEN, D_MODEL, IN_PROJ_COLS), D_MODEL),
        "hyb_conv_w": nrm(ks[6], (N_EVEN, CONV_W, CONV_CH), CONV_W),
        "hyb_decay_fwd": base_decay[None] + 0.05 * jax.random.normal(ks[7], (N_EVEN, RET_HEADS), F32),
        "hyb_decay_bwd": base_decay[None] + 0.05 * jax.random.normal(ks[8], (N_EVEN, RET_HEADS), F32),
        "hyb_gn": gain(ks[9], (N_EVEN, RET_WIDTH)),
        "hyb_w_out": nrm(ks[10], (N_EVEN, CONV_CH + RET_WIDTH, D_MODEL), CONV_CH + RET_WIDTH),
        "diff_w_qkv": nrm(ks[11], (N_ODD, D_MODEL, 2 * DIFF_QK + DIFF_V), D_MODEL),
        "diff_lq1": 0.1 * jax.random.normal(ks[12], (N_ODD, DIFF_DIM), F32),
        "diff_lk1": 0.1 * jax.random.normal(ks[13], (N_ODD, DIFF_DIM), F32),
        "diff_lq2": 0.1 * jax.random.normal(ks[14], (N_ODD, DIFF_DIM), F32),
        "diff_lk2": 0.1 * jax.random.normal(ks[15], (N_ODD, DIFF_DIM), F32),
        "diff_subln": gain(ks[16], (N_ODD, 2 * DIFF_DIM)),
        "diff_w_out": nrm(ks[17], (N_ODD, DIFF_V, D_MODEL), DIFF_V),
        "ffn_w_gate": nrm(ks[18], (DEPTH, D_MODEL, D_FF), D_MODEL),
        "ffn_w_up": nrm(ks[19], (DEPTH, D_MODEL, D_FF), D_MODEL),
        "ffn_w_down": nrm(ks[20], (DEPTH, D_FF, D_MODEL), D_FF),
    }


def reference(x_prompt, x_sample, norm_mix, norm_ffn, norm_final, hyb_w_in, hyb_conv_w,
              hyb_decay_fwd, hyb_decay_bwd, hyb_gn, hyb_w_out, diff_w_qkv, diff_lq1, diff_lk1,
              diff_lq2, diff_lk2, diff_subln, diff_w_out, ffn_w_gate, ffn_w_up, ffn_w_down):
    y_prompt = _trunk(x_prompt, norm_mix, norm_ffn, norm_final, hyb_w_in, hyb_conv_w, hyb_decay_fwd,
                      hyb_decay_bwd, hyb_gn, hyb_w_out, diff_w_qkv, diff_lq1, diff_lk1, diff_lq2,
                      diff_lk2, diff_subln, diff_w_out, ffn_w_gate, ffn_w_up, ffn_w_down)
    y_sample = _trunk(x_sample, norm_mix, norm_ffn, norm_final, hyb_w_in, hyb_conv_w, hyb_decay_fwd,
                      hyb_decay_bwd, hyb_gn, hyb_w_out, diff_w_qkv, diff_lq1, diff_lk1, diff_lq2,
                      diff_lk2, diff_subln, diff_w_out, ffn_w_gate, ffn_w_up, ffn_w_down)
    return (y_prompt, y_sample)
```

```python
import functools
import math

import jax
import jax.numpy as jnp
from jax import lax
from jax.experimental import pallas as pl
from jax.experimental.pallas import tpu as pltpu

F32 = jnp.float32
BF16 = jnp.bfloat16

D_MODEL = 1024
CONV_CH = 512
RET_HEADS = 4
RET_DIM = 128
RET_WIDTH = RET_HEADS * RET_DIM
RET_CHUNK = 128
DIFF_HEADS = 8
DIFF_DIM = 64
D_FF = 2816
ROPE_THETA = 10000.0
NORM_EPS = 1e-6
GN_EPS = 1e-5
SUBLN_EPS = 1e-5
IN_PROJ_COLS = 3 * CONV_CH + 4 * RET_WIDTH

LANES = 128
SUBLANES = 8
COL_CHUNK = 512
FF_CHUNK = 256
ROW_TILE = 512
ATT_TQ = 512
ATT_TK = 512
VMEM_LIMIT = 56 * 1024 * 1024


def _rms(x, w, eps):
    ms = jnp.mean(x * x, axis=-1, keepdims=True)
    return (x * lax.rsqrt(ms + eps)) * w


def _rope_tables(seq, dim):
    half = dim // 2
    inv = ROPE_THETA ** (-jnp.arange(0, dim, 2, dtype=F32) / dim)
    ang = jnp.arange(seq, dtype=F32)[:, None] * inv[None, :]
    cos = jnp.cos(ang)
    sin = jnp.sin(ang)
    zero = jnp.zeros_like(sin)
    reps = LANES // dim
    cos_t = jnp.tile(jnp.concatenate([cos, cos], axis=1), (1, reps))
    sin_lo = jnp.tile(jnp.concatenate([-sin, zero], axis=1), (1, reps))
    sin_hi = jnp.tile(jnp.concatenate([zero, sin], axis=1), (1, reps))
    return cos_t, sin_lo, sin_hi


def _rope128(y, cos, sin_lo, sin_hi, half):
    return (y * cos + pltpu.roll(y, LANES - half, 1) * sin_lo + pltpu.roll(y, half, 1) * sin_hi)


def _hyb_in_kernel(x_ref, nw_ref, w_ref, cos_ref, slo_ref, shi_ref, o_ref):
    xn = _rms(x_ref[...], nw_ref[...], NORM_EPS).astype(BF16)
    cos, slo, shi = cos_ref[...], slo_ref[...], shi_ref[...]
    q_chunk = 3 * CONV_CH // COL_CHUNK
    for c in range(IN_PROJ_COLS // COL_CHUNK):
        y = jnp.dot(xn, w_ref[:, c * COL_CHUNK:(c + 1) * COL_CHUNK], preferred_element_type=F32)
        if c in (q_chunk, q_chunk + 1):
            scale = 1.0 if c == q_chunk else RET_DIM ** -0.5
            for g in range(COL_CHUNK // LANES):
                r = _rope128(y[:, g * LANES:(g + 1) * LANES], cos, slo, shi, RET_DIM // 2)
                if scale != 1.0:
                    r = r * scale
                o_ref[:, c * COL_CHUNK + g * LANES:c * COL_CHUNK + (g + 1) * LANES] = r.astype(BF16)
        else:
            o_ref[:, c * COL_CHUNK:(c + 1) * COL_CHUNK] = y.astype(BF16)


def _hyb_in(x2, nw, w_in, tabs, seq, tm):
    t = x2.shape[0]
    spt = seq // tm
    row = lambda i: (i, 0)
    pos = lambda i: (i % spt, 0)
    const = lambda i: (0, 0)
    return pl.pallas_call(
        _hyb_in_kernel,
        grid=(t // tm,),
        in_specs=[pl.BlockSpec((tm, D_MODEL), row),
                  pl.BlockSpec((1, D_MODEL), const),
                  pl.BlockSpec((D_MODEL, IN_PROJ_COLS), const),
                  pl.BlockSpec((tm, LANES), pos),
                  pl.BlockSpec((tm, LANES), pos),
                  pl.BlockSpec((tm, LANES), pos)],
        out_specs=pl.BlockSpec((tm, IN_PROJ_COLS), row),
        out_shape=jax.ShapeDtypeStruct((t, IN_PROJ_COLS), BF16),
        compiler_params=pltpu.CompilerParams(dimension_semantics=("arbitrary",),
                                             vmem_limit_bytes=VMEM_LIMIT),
        name="hyb_in",
    )(x2, nw, w_in, *tabs)


def _decay_tables(lg, strict):
    c = RET_CHUNK
    ii = lax.broadcasted_iota(jnp.int32, (c, c), 0).astype(F32)
    jj = lax.broadcasted_iota(jnp.int32, (c, c), 1).astype(F32)
    if strict:
        rel = jj - ii
        mask = rel > 0
        kdec = jnp.exp(lg * ii)
        qdec = jnp.exp(lg * (c - ii))
    else:
        rel = ii - jj
        mask = rel >= 0
        kdec = jnp.exp(lg * (c - 1.0 - ii))
        qdec = jnp.exp(lg * (ii + 1.0))
    dintra = jnp.where(mask, jnp.exp(lg * jnp.maximum(rel, 0.0)), 0.0)
    return dintra, kdec, qdec, jnp.exp(lg * c)


def _ret_chunk(q, k, v, st, dintra, kdec, qdec, cdec):
    sc = lax.dot_general(q, k, (((1,), (1,)), ((), ())), preferred_element_type=F32) * dintra
    o = jnp.dot(sc.astype(BF16), v, preferred_element_type=F32)
    o = o + jnp.dot(q, st.astype(BF16), preferred_element_type=F32) * qdec
    kd_t = (k.astype(F32) * kdec).T.astype(BF16)
    kv = jnp.dot(kd_t, v, preferred_element_type=F32)
    return o, st * cdec + kv


def _ret_fwd_kernel(q_ref, k_ref, v_ref, dec_ref, of_ref, state_sc, *, n_chunks):
    @pl.when(pl.program_id(1) == 0)
    def _():
        state_sc[...] = jnp.zeros_like(state_sc)

    for h in range(RET_HEADS):
        lg = -jnp.exp(dec_ref[:, h:h + 1])
        tabs = _decay_tables(lg, strict=False)
        cols = slice(h * RET_DIM, (h + 1) * RET_DIM)
        for c in range(n_chunks):
            rows = slice(c * RET_CHUNK, (c + 1) * RET_CHUNK)
            o, st = _ret_chunk(q_ref[rows, cols], k_ref[rows, cols], v_ref[rows, cols], state_sc[h], *tabs)
            of_ref[rows, cols] = o
            state_sc[h] = st


def _ret_fwd(proj, dec, batch, seq, tc):
    t = proj.shape[0]
    nsteps = seq // tc
    qb = 3 * CONV_CH // RET_WIDTH

    def blk(col):
        return pl.BlockSpec((tc, RET_WIDTH), lambda b, i: (b * nsteps + i, col))

    return pl.pallas_call(
        functools.partial(_ret_fwd_kernel, n_chunks=tc // RET_CHUNK),
        grid=(batch, nsteps),
        in_specs=[blk(qb), blk(qb + 1), blk(qb + 2), pl.BlockSpec((1, RET_HEADS), lambda b, i: (0, 0))],
        out_specs=pl.BlockSpec((tc, RET_WIDTH), lambda b, i: (b * nsteps + i, 0)),
        out_shape=jax.ShapeDtypeStruct((t, RET_WIDTH), F32),
        scratch_shapes=[pltpu.VMEM((RET_HEADS, RET_DIM, RET_DIM), F32)],
        compiler_params=pltpu.CompilerParams(dimension_semantics=("arbitrary", "arbitrary"),
                                             vmem_limit_bytes=VMEM_LIMIT),
        name="ret_fwd",
    )(proj, proj, proj, dec)


def _hyb_out_kernel(x_ref, ab_ref, ac_ref, ah_ref, q_ref, k_ref, v_ref, g_ref, of_ref,
                    acp_ref, ahp_ref, acn_ref, ahn_ref, dec_ref, cw_ref, gn_ref, wo_ref,
                    o_ref, state_sc, y_sc, *, n_chunks, nsteps):
    step = pl.program_id(1)

    @pl.when(step == 0)
    def _():
        state_sc[...] = jnp.zeros_like(state_sc)

    for h in range(RET_HEADS):
        lg = -jnp.exp(dec_ref[:, h:h + 1])
        tabs = _decay_tables(lg, strict=True)
        cols = slice(h * RET_DIM, (h + 1) * RET_DIM)
        gn_w = gn_ref[:, cols]
        for c in reversed(range(n_chunks)):
            rows = slice(c * RET_CHUNK, (c + 1) * RET_CHUNK)
            o, st = _ret_chunk(q_ref[rows, cols], k_ref[rows, cols], v_ref[rows, cols], state_sc[h], *tabs)
            state_sc[h] = st
            o = o + of_ref[rows, cols]
            mu = jnp.mean(o, axis=-1, keepdims=True)
            d = o - mu
            var = jnp.mean(d * d, axis=-1, keepdims=True)
            on = d * lax.rsqrt(var + GN_EPS) * gn_w
            g = g_ref[rows, cols].astype(F32)
            y_sc[rows, CONV_CH + h * RET_DIM:CONV_CH + (h + 1) * RET_DIM] = (jax.nn.silu(g) * on).astype(BF16)

    tc = n_chunks * RET_CHUNK
    tile = nsteps - 1 - step
    u = ac_ref[...].astype(F32) * ah_ref[...].astype(F32)
    u_prev = acp_ref[SUBLANES - 1:SUBLANES, :].astype(F32) * ahp_ref[SUBLANES - 1:SUBLANES, :].astype(F32)
    u_next = acn_ref[0:1, :].astype(F32) * ahn_ref[0:1, :].astype(F32)
    u_prev = jnp.where(tile > 0, u_prev, 0.0)
    u_next = jnp.where(tile < nsteps - 1, u_next, 0.0)
    ridx = lax.broadcasted_iota(jnp.int32, u.shape, 0)
    up = jnp.where(ridx == 0, u_prev, pltpu.roll(u, 1, 0))
    un = jnp.where(ridx == tc - 1, u_next, pltpu.roll(u, tc - 1, 0))
    conv = cw_ref[0:1, :] * up + cw_ref[1:2, :] * u + cw_ref[2:3, :] * un
    y_sc[:, 0:CONV_CH] = (ab_ref[...].astype(F32) * conv).astype(BF16)

    o_ref[...] = x_ref[...] + jnp.dot(y_sc[...], wo_ref[...], preferred_element_type=F32)


def _hyb_out(x2, proj, o_f, dec, conv_w, gn_w, w_out, batch, seq, tc):
    t = x2.shape[0]
    nsteps = seq // tc
    rpt = tc // SUBLANES
    n8 = t // SUBLANES
    tile = lambda b, i: b * nsteps + (nsteps - 1 - i)

    def blk(col, width=RET_WIDTH):
        return pl.BlockSpec((tc, width), lambda b, i: (tile(b, i), col))

    def halo_prev(col):
        return pl.BlockSpec((SUBLANES, CONV_CH), lambda b, i: (jnp.maximum(tile(b, i) * rpt - 1, 0), col))

    def halo_next(col):
        return pl.BlockSpec((SUBLANES, CONV_CH), lambda b, i: (jnp.minimum((tile(b, i) + 1) * rpt, n8 - 1), col))

    const = lambda b, i: (0, 0)
    return pl.pallas_call(
        functools.partial(_hyb_out_kernel, n_chunks=tc // RET_CHUNK, nsteps=nsteps),
        grid=(batch, nsteps),
        in_specs=[blk(0, D_MODEL),
                  blk(0), blk(1), blk(2), blk(3), blk(4), blk(5), blk(6), blk(0),
                  halo_prev(1), halo_prev(2), halo_next(1), halo_next(2),
                  pl.BlockSpec((1, RET_HEADS), const),
                  pl.BlockSpec((3, CONV_CH), const),
                  pl.BlockSpec((1, RET_WIDTH), const),
                  pl.BlockSpec((CONV_CH + RET_WIDTH, D_MODEL), const)],
        out_specs=blk(0, D_MODEL),
        out_shape=jax.ShapeDtypeStruct((t, D_MODEL), F32),
        scratch_shapes=[pltpu.VMEM((RET_HEADS, RET_DIM, RET_DIM), F32),
                        pltpu.VMEM((tc, CONV_CH + RET_WIDTH), BF16)],
        compiler_params=pltpu.CompilerParams(dimension_semantics=("arbitrary", "arbitrary"),
                                             vmem_limit_bytes=VMEM_LIMIT),
        name="hyb_out",
    )(x2, proj, proj, proj, proj, proj, proj, proj, o_f, proj, proj, proj, proj, dec, conv_w, gn_w, w_out)


def _ffn_kernel(*refs, pre_proj, final_norm):
    refs = list(refs)
    x_ref = refs.pop(0)
    if pre_proj:
        a_ref, wo_ref = refs.pop(0), refs.pop(0)
    nw_ref, wg_ref, wu_ref, wd_ref = refs[:4]
    refs = refs[4:]
    if final_norm:
        fw_ref = refs.pop(0)
    o_ref = refs.pop(0)

    x = x_ref[...]
    if pre_proj:
        x = x + jnp.dot(a_ref[...], wo_ref[...], preferred_element_type=F32)
    xn = _rms(x, nw_ref[...], NORM_EPS).astype(BF16)
    acc = x
    for c in range(D_FF // FF_CHUNK):
        cs = slice(c * FF_CHUNK, (c + 1) * FF_CHUNK)
        g = jnp.dot(xn, wg_ref[:, cs], preferred_element_type=F32)
        u = jnp.dot(xn, wu_ref[:, cs], preferred_element_type=F32)
        h = (jax.nn.silu(g) * u).astype(BF16)
        acc = acc + jnp.dot(h, wd_ref[cs, :], preferred_element_type=F32)
    if final_norm:
        acc = _rms(acc, fw_ref[...], NORM_EPS)
    o_ref[...] = acc


def _ffn(x2, nw, wg, wu, wd, tm, pre=None, final_w=None):
    t = x2.shape[0]
    row = lambda i: (i, 0)
    const = lambda i: (0, 0)
    single = pl.Buffered(1)
    args = [x2]
    specs = [pl.BlockSpec((tm, D_MODEL), row)]
    if pre is not None:
        a, wo = pre
        args += [a, wo]
        specs += [pl.BlockSpec((tm, D_MODEL), row),
                  pl.BlockSpec((D_MODEL, D_MODEL), const, pipeline_mode=single)]
    args += [nw, wg, wu, wd]
    specs += [pl.BlockSpec((1, D_MODEL), const),
              pl.BlockSpec((D_MODEL, D_FF), const, pipeline_mode=single),
              pl.BlockSpec((D_MODEL, D_FF), const, pipeline_mode=single),
              pl.BlockSpec((D_FF, D_MODEL), const, pipeline_mode=single)]
    if final_w is not None:
        args.append(final_w)
        specs.append(pl.BlockSpec((1, D_MODEL), const))
    return pl.pallas_call(
        functools.partial(_ffn_kernel, pre_proj=pre is not None, final_norm=final_w is not None),
        grid=(t // tm,),
        in_specs=specs,
        out_specs=pl.BlockSpec((tm, D_MODEL), row),
        out_shape=jax.ShapeDtypeStruct((t, D_MODEL), F32),
        compiler_params=pltpu.CompilerParams(dimension_semantics=("arbitrary",),
                                             vmem_limit_bytes=VMEM_LIMIT),
        name="ffn",
    )(*args)


def _qkv_kernel(x_ref, nw_ref, w_ref, cos_ref, slo_ref, shi_ref, qt_ref, k_ref, vt_ref):
    xn = _rms(x_ref[...], nw_ref[...], NORM_EPS).astype(BF16)
    cos, slo, shi = cos_ref[...], slo_ref[...], shi_ref[...]
    n_qk = D_MODEL // COL_CHUNK
    for c in range(3 * n_qk):
        y = jnp.dot(xn, w_ref[:, c * COL_CHUNK:(c + 1) * COL_CHUNK], preferred_element_type=F32)
        kind, cc = divmod(c, n_qk)
        for g in range(COL_CHUNK // LANES):
            yg = y[:, g * LANES:(g + 1) * LANES]
            lo = cc * COL_CHUNK + g * LANES
            if kind == 0:
                r = _rope128(yg, cos, slo, shi, DIFF_DIM // 2) * (DIFF_DIM ** -0.5)
                qt_ref[0, lo:lo + LANES, :] = r.T.astype(BF16)
            elif kind == 1:
                k_ref[:, lo:lo + LANES] = _rope128(yg, cos, slo, shi, DIFF_DIM // 2).astype(BF16)
            else:
                vt_ref[0, 0, lo:lo + LANES, :] = yg.T.astype(BF16)


def _qkv(x2, nw, w_qkv, tabs, batch, seq, tm):
    t = x2.shape[0]
    spt = seq // tm
    row = lambda i: (i, 0)
    pos = lambda i: (i % spt, 0)
    const = lambda i: (0, 0)
    return pl.pallas_call(
        _qkv_kernel,
        grid=(t // tm,),
        in_specs=[pl.BlockSpec((tm, D_MODEL), row),
                  pl.BlockSpec((1, D_MODEL), const),
                  pl.BlockSpec((D_MODEL, 3 * D_MODEL), const),
                  pl.BlockSpec((tm, LANES), pos),
                  pl.BlockSpec((tm, LANES), pos),
                  pl.BlockSpec((tm, LANES), pos)],
        out_specs=[pl.BlockSpec((1, D_MODEL, tm), lambda i: (i // spt, 0, i % spt)),
                   pl.BlockSpec((tm, D_MODEL), row),
                   pl.BlockSpec((1, 1, D_MODEL, tm), lambda i: (i // spt, i % spt, 0, 0))],
        out_shape=[jax.ShapeDtypeStruct((batch, D_MODEL, seq), BF16),
                   jax.ShapeDtypeStruct((t, D_MODEL), BF16),
                   jax.ShapeDtypeStruct((batch, spt, D_MODEL, tm), BF16)],
        compiler_params=pltpu.CompilerParams(dimension_semantics=("arbitrary",),
                                             vmem_limit_bytes=VMEM_LIMIT),
        name="qkv",
    )(x2, nw, w_qkv, *tabs)


def _attn_kernel(qt_ref, k_ref, vt_ref, lq1_ref, lk1_ref, lq2_ref, lk2_ref, sub_ref, o_ref,
                 m_sc, l_sc, acc_sc, *, tk, nk, lambda_init):
    qt = qt_ref[0]
    rowi = lax.broadcasted_iota(jnp.int32, qt.shape, 0)
    zero = jnp.zeros_like(qt)
    qz = (jnp.where(rowi < DIFF_DIM, qt, zero), jnp.where(rowi >= DIFF_DIM, qt, zero))
    m_sc[...] = jnp.full_like(m_sc, -jnp.inf)
    l_sc[...] = jnp.zeros_like(l_sc)
    acc_sc[...] = jnp.zeros_like(acc_sc)

    def body(j, carry):
        off = pl.multiple_of(j * tk, tk)
        k = k_ref[pl.ds(off, tk), :]
        vt = vt_ref[0, j]
        for idx in range(2):
            st = jnp.dot(k, qz[idx], preferred_element_type=F32)
            m_prev = m_sc[idx]
            m_new = jnp.maximum(m_prev, jnp.max(st, axis=0, keepdims=True))
            alpha = jnp.exp(m_prev - m_new)
            p = jnp.exp(st - m_new)
            l_sc[idx] = alpha * l_sc[idx] + jnp.sum(p, axis=0, keepdims=True)
            acc_sc[idx] = acc_sc[idx] * alpha + jnp.dot(vt, p.astype(BF16), preferred_element_type=F32)
            m_sc[idx] = m_new
        return carry

    lax.fori_loop(0, nk, body, 0)

    lam = (jnp.exp(jnp.sum(lq1_ref[...] * lk1_ref[...], axis=-1, keepdims=True))
           - jnp.exp(jnp.sum(lq2_ref[...] * lk2_ref[...], axis=-1, keepdims=True)) + lambda_init)
    o = acc_sc[0] / l_sc[0] - lam * (acc_sc[1] / l_sc[1])
    ms = jnp.mean(o * o, axis=0, keepdims=True)
    o = (o * lax.rsqrt(ms + SUBLN_EPS)).T
    o_ref[...] = (o * sub_ref[...] * (1.0 - lambda_init)).astype(BF16)


def _attn(qt, k2, vt, lq1, lk1, lq2, lk2, subln, batch, seq, tq, tk, lambda_init):
    nq = seq // tq
    nk = seq // tk
    hd = 2 * DIFF_DIM
    const = lambda b, h, i: (0, 0)
    return pl.pallas_call(
        functools.partial(_attn_kernel, tk=tk, nk=nk, lambda_init=lambda_init),
        grid=(batch, DIFF_HEADS, nq),
        in_specs=[pl.BlockSpec((1, hd, tq), lambda b, h, i: (b, h, i)),
                  pl.BlockSpec((seq, hd), lambda b, h, i: (b, h)),
                  pl.BlockSpec((1, nk, hd, tk), lambda b, h, i: (b, 0, h, 0)),
                  pl.BlockSpec((1, DIFF_DIM), const), pl.BlockSpec((1, DIFF_DIM), const),
                  pl.BlockSpec((1, DIFF_DIM), const), pl.BlockSpec((1, DIFF_DIM), const),
                  pl.BlockSpec((1, hd), const)],
        out_specs=pl.BlockSpec((tq, hd), lambda b, h, i: (b * nq + i, h)),
        out_shape=jax.ShapeDtypeStruct((batch * seq, DIFF_HEADS * hd), BF16),
        scratch_shapes=[pltpu.VMEM((2, 1, tq), F32), pltpu.VMEM((2, 1, tq), F32),
                        pltpu.VMEM((2, hd, tq), F32)],
        compiler_params=pltpu.CompilerParams(dimension_semantics=("arbitrary", "arbitrary", "arbitrary"),
                                             vmem_limit_bytes=VMEM_LIMIT),
        name="attn",
    )(qt, k2, vt, lq1, lk1, lq2, lk2, subln)


def _trunk(x, p):
    batch, seq, _ = x.shape
    t = batch * seq
    tm = min(ROW_TILE, seq)
    tq = min(ATT_TQ, seq)
    x2 = x.reshape(t, D_MODEL)

    proj = _hyb_in(x2, p["norm_mix"][0:1], p["hyb_w_in"], _rope_tables(seq, RET_DIM), seq, tm)
    o_f = _ret_fwd(proj, p["hyb_decay_fwd"], batch, seq, tm)
    x2 = _hyb_out(x2, proj, o_f, p["hyb_decay_bwd"], p["hyb_conv_w"], p["hyb_gn"], p["hyb_w_out"],
                  batch, seq, tm)
    x2 = _ffn(x2, p["norm_ffn"][0:1], p["ffn_w_gate"][0], p["ffn_w_up"][0], p["ffn_w_down"][0], tm)

    lambda_init = 0.8 - 0.6 * math.exp(-0.3 * 1)
    qt, k2, vt = _qkv(x2, p["norm_mix"][1:2], p["diff_w_qkv"], _rope_tables(seq, DIFF_DIM), batch, seq, tm)
    att = _attn(qt, k2, vt, p["diff_lq1"], p["diff_lk1"], p["diff_lq2"], p["diff_lk2"], p["diff_subln"],
                batch, seq, tq, tm, lambda_init)
    x2 = _ffn(x2, p["norm_ffn"][1:2], p["ffn_w_gate"][1], p["ffn_w_up"][1], p["ffn_w_down"][1], tm,
              pre=(att, p["diff_w_out"]), final_w=p["norm_final"])
    return x2.reshape(batch, seq, D_MODEL)


def kernel(x_prompt, x_sample, norm_mix, norm_ffn, norm_final, hyb_w_in, hyb_conv_w, hyb_decay_fwd,
           hyb_decay_bwd, hyb_gn, hyb_w_out, diff_w_qkv, diff_lq1, diff_lk1, diff_lq2, diff_lk2,
           diff_subln, diff_w_out, ffn_w_gate, ffn_w_up, ffn_w_down):
    assert norm_mix.shape[0] == 2 and hyb_w_in.shape[0] == 1 and diff_w_qkv.shape[0] == 1
    p = {
        "norm_mix": norm_mix, "norm_ffn": norm_ffn, "norm_final": norm_final.reshape(1, D_MODEL),
        "hyb_w_in": hyb_w_in[0].astype(BF16), "hyb_conv_w": hyb_conv_w[0],
        "hyb_decay_fwd": hyb_decay_fwd, "hyb_decay_bwd": hyb_decay_bwd,
        "hyb_gn": hyb_gn, "hyb_w_out": hyb_w_out[0].astype(BF16),
        "diff_w_qkv": diff_w_qkv[0].astype(BF16),
        "diff_lq1": diff_lq1, "diff_lk1": diff_lk1, "diff_lq2": diff_lq2, "diff_lk2": diff_lk2,
        "diff_subln": diff_subln, "diff_w_out": diff_w_out[0].astype(BF16),
        "ffn_w_gate": ffn_w_gate.astype(BF16), "ffn_w_up": ffn_w_up.astype(BF16),
        "ffn_w_down": ffn_w_down.astype(BF16),
    }
    return (_trunk(x_prompt, p), _trunk(x_sample, p))
```

```python
import functools
import math

import jax
import jax.numpy as jnp
from jax import lax
from jax.experimental import pallas as pl
from jax.experimental.pallas import tpu as pltpu

F32 = jnp.float32
BF16 = jnp.bfloat16

D_MODEL = 1024
CONV_CH = 512
RET_HEADS = 4
RET_DIM = 128
RET_WIDTH = RET_HEADS * RET_DIM
RET_CHUNK = 128
DIFF_HEADS = 8
DIFF_DIM = 64
D_FF = 2816
ROPE_THETA = 10000.0
NORM_EPS = 1e-6
LOG2E = math.log2(math.e)
GN_EPS = 1e-5
SUBLN_EPS = 1e-5
IN_PROJ_COLS = 3 * CONV_CH + 4 * RET_WIDTH

LANES = 128
SUBLANES = 8
COL_CHUNK = 512
FF_CHUNK = 256
ROW_TILE = 512
ATT_TQ = 512
ATT_TK = 512
VMEM_LIMIT = 56 * 1024 * 1024


def _rms(x, w, eps):
    ms = jnp.mean(x * x, axis=-1, keepdims=True)
    return (x * lax.rsqrt(ms + eps)) * w


def _rope_tables(seq, dim):
    half = dim // 2
    inv = ROPE_THETA ** (-jnp.arange(0, dim, 2, dtype=F32) / dim)
    ang = jnp.arange(seq, dtype=F32)[:, None] * inv[None, :]
    cos = jnp.cos(ang)
    sin = jnp.sin(ang)
    zero = jnp.zeros_like(sin)
    reps = LANES // dim
    cos_t = jnp.tile(jnp.concatenate([cos, cos], axis=1), (1, reps))
    sin_lo = jnp.tile(jnp.concatenate([-sin, zero], axis=1), (1, reps))
    sin_hi = jnp.tile(jnp.concatenate([zero, sin], axis=1), (1, reps))
    return cos_t, sin_lo, sin_hi


def _rope128(y, cos, sin_lo, sin_hi, half):
    return (y * cos + pltpu.roll(y, LANES - half, 1) * sin_lo + pltpu.roll(y, half, 1) * sin_hi)


def _hyb_in_kernel(x_ref, nw_ref, w_ref, cos_ref, slo_ref, shi_ref, o_ref):
    xn = _rms(x_ref[...], nw_ref[...], NORM_EPS).astype(BF16)
    cos, slo, shi = cos_ref[...], slo_ref[...], shi_ref[...]
    q_chunk = 3 * CONV_CH // COL_CHUNK
    for c in range(IN_PROJ_COLS // COL_CHUNK):
        y = jnp.dot(xn, w_ref[:, c * COL_CHUNK:(c + 1) * COL_CHUNK], preferred_element_type=F32)
        if c in (q_chunk, q_chunk + 1):
            scale = 1.0 if c == q_chunk else RET_DIM ** -0.5
            for g in range(COL_CHUNK // LANES):
                r = _rope128(y[:, g * LANES:(g + 1) * LANES], cos, slo, shi, RET_DIM // 2)
                if scale != 1.0:
                    r = r * scale
                o_ref[:, c * COL_CHUNK + g * LANES:c * COL_CHUNK + (g + 1) * LANES] = r.astype(BF16)
        else:
            o_ref[:, c * COL_CHUNK:(c + 1) * COL_CHUNK] = y.astype(BF16)


def _hyb_in(x2, nw, w_in, tabs, seq, tm):
    t = x2.shape[0]
    spt = seq // tm
    row = lambda i: (i, 0)
    pos = lambda i: (i % spt, 0)
    const = lambda i: (0, 0)
    return pl.pallas_call(
        _hyb_in_kernel,
        grid=(t // tm,),
        in_specs=[pl.BlockSpec((tm, D_MODEL), row),
                  pl.BlockSpec((1, D_MODEL), const),
                  pl.BlockSpec((D_MODEL, IN_PROJ_COLS), const),
                  pl.BlockSpec((tm, LANES), pos),
                  pl.BlockSpec((tm, LANES), pos),
                  pl.BlockSpec((tm, LANES), pos)],
        out_specs=pl.BlockSpec((tm, IN_PROJ_COLS), row),
        out_shape=jax.ShapeDtypeStruct((t, IN_PROJ_COLS), BF16),
        compiler_params=pltpu.CompilerParams(dimension_semantics=("arbitrary",),
                                             vmem_limit_bytes=VMEM_LIMIT),
        name="hyb_in",
    )(x2, nw, w_in, *tabs)


def _decay_tables(lg, strict):
    c = RET_CHUNK
    ii = lax.broadcasted_iota(jnp.int32, (c, c), 0).astype(F32)
    jj = lax.broadcasted_iota(jnp.int32, (c, c), 1).astype(F32)
    if strict:
        rel = jj - ii
        mask = rel > 0
        kdec = jnp.exp(lg * ii)
        qdec = jnp.exp(lg * (c - ii))
    else:
        rel = ii - jj
        mask = rel >= 0
        kdec = jnp.exp(lg * (c - 1.0 - ii))
        qdec = jnp.exp(lg * (ii + 1.0))
    dintra = jnp.where(mask, jnp.exp(lg * jnp.maximum(rel, 0.0)), 0.0)
    return dintra, kdec, qdec, jnp.exp(lg * c)


def _ret_chunk(q, k, v, st, dintra, kdec, qdec, cdec):
    sc = lax.dot_general(q, k, (((1,), (1,)), ((), ())), preferred_element_type=F32) * dintra
    o = jnp.dot(sc.astype(BF16), v, preferred_element_type=F32)
    o = o + jnp.dot(q, st.astype(BF16), preferred_element_type=F32) * qdec
    kd_t = (k.astype(F32) * kdec).T.astype(BF16)
    kv = jnp.dot(kd_t, v, preferred_element_type=F32)
    return o, st * cdec + kv


def _ret_fwd_kernel(q_ref, k_ref, v_ref, dec_ref, of_ref, state_sc, *, n_chunks):
    @pl.when(pl.program_id(1) == 0)
    def _():
        state_sc[...] = jnp.zeros_like(state_sc)

    for h in range(RET_HEADS):
        lg = -jnp.exp(dec_ref[:, h:h + 1])
        tabs = _decay_tables(lg, strict=False)
        cols = slice(h * RET_DIM, (h + 1) * RET_DIM)
        for c in range(n_chunks):
            rows = slice(c * RET_CHUNK, (c + 1) * RET_CHUNK)
            o, st = _ret_chunk(q_ref[rows, cols], k_ref[rows, cols], v_ref[rows, cols], state_sc[h], *tabs)
            of_ref[rows, cols] = o
            state_sc[h] = st


def _ret_fwd(proj, dec, batch, seq, tc):
    t = proj.shape[0]
    nsteps = seq // tc
    qb = 3 * CONV_CH // RET_WIDTH

    def blk(col):
        return pl.BlockSpec((tc, RET_WIDTH), lambda b, i: (b * nsteps + i, col))

    return pl.pallas_call(
        functools.partial(_ret_fwd_kernel, n_chunks=tc // RET_CHUNK),
        grid=(batch, nsteps),
        in_specs=[blk(qb), blk(qb + 1), blk(qb + 2), pl.BlockSpec((1, RET_HEADS), lambda b, i: (0, 0))],
        out_specs=pl.BlockSpec((tc, RET_WIDTH), lambda b, i: (b * nsteps + i, 0)),
        out_shape=jax.ShapeDtypeStruct((t, RET_WIDTH), F32),
        scratch_shapes=[pltpu.VMEM((RET_HEADS, RET_DIM, RET_DIM), F32)],
        compiler_params=pltpu.CompilerParams(dimension_semantics=("arbitrary", "arbitrary"),
                                             vmem_limit_bytes=VMEM_LIMIT),
        name="ret_fwd",
    )(proj, proj, proj, dec)


def _hyb_out_kernel(x_ref, ab_ref, ac_ref, ah_ref, q_ref, k_ref, v_ref, g_ref, of_ref,
                    acp_ref, ahp_ref, acn_ref, ahn_ref, dec_ref, cw_ref, gn_ref, wo_ref,
                    o_ref, state_sc, y_sc, *, n_chunks, nsteps):
    step = pl.program_id(1)

    @pl.when(step == 0)
    def _():
        state_sc[...] = jnp.zeros_like(state_sc)

    for h in range(RET_HEADS):
        lg = -jnp.exp(dec_ref[:, h:h + 1])
        tabs = _decay_tables(lg, strict=True)
        cols = slice(h * RET_DIM, (h + 1) * RET_DIM)
        gn_w = gn_ref[:, cols]
        for c in reversed(range(n_chunks)):
            rows = slice(c * RET_CHUNK, (c + 1) * RET_CHUNK)
            o, st = _ret_chunk(q_ref[rows, cols], k_ref[rows, cols], v_ref[rows, cols], state_sc[h], *tabs)
            state_sc[h] = st
            o = o + of_ref[rows, cols]
            mu = jnp.mean(o, axis=-1, keepdims=True)
            d = o - mu
            var = jnp.mean(d * d, axis=-1, keepdims=True)
            on = d * lax.rsqrt(var + GN_EPS) * gn_w
            g = g_ref[rows, cols].astype(F32)
            y_sc[rows, CONV_CH + h * RET_DIM:CONV_CH + (h + 1) * RET_DIM] = (jax.nn.silu(g) * on).astype(BF16)

    tc = n_chunks * RET_CHUNK
    tile = nsteps - 1 - step
    u = ac_ref[...].astype(F32) * ah_ref[...].astype(F32)
    u_prev = acp_ref[SUBLANES - 1:SUBLANES, :].astype(F32) * ahp_ref[SUBLANES - 1:SUBLANES, :].astype(F32)
    u_next = acn_ref[0:1, :].astype(F32) * ahn_ref[0:1, :].astype(F32)
    u_prev = jnp.where(tile > 0, u_prev, 0.0)
    u_next = jnp.where(tile < nsteps - 1, u_next, 0.0)
    ridx = lax.broadcasted_iota(jnp.int32, u.shape, 0)
    up = jnp.where(ridx == 0, u_prev, pltpu.roll(u, 1, 0))
    un = jnp.where(ridx == tc - 1, u_next, pltpu.roll(u, tc - 1, 0))
    conv = cw_ref[0:1, :] * up + cw_ref[1:2, :] * u + cw_ref[2:3, :] * un
    y_sc[:, 0:CONV_CH] = (ab_ref[...].astype(F32) * conv).astype(BF16)

    o_ref[...] = x_ref[...] + jnp.dot(y_sc[...], wo_ref[...], preferred_element_type=F32)


def _hyb_out(x2, proj, o_f, dec, conv_w, gn_w, w_out, batch, seq, tc):
    t = x2.shape[0]
    nsteps = seq // tc
    rpt = tc // SUBLANES
    n8 = t // SUBLANES
    tile = lambda b, i: b * nsteps + (nsteps - 1 - i)

    def blk(col, width=RET_WIDTH):
        return pl.BlockSpec((tc, width), lambda b, i: (tile(b, i), col))

    def halo_prev(col):
        return pl.BlockSpec((SUBLANES, CONV_CH), lambda b, i: (jnp.maximum(tile(b, i) * rpt - 1, 0), col))

    def halo_next(col):
        return pl.BlockSpec((SUBLANES, CONV_CH), lambda b, i: (jnp.minimum((tile(b, i) + 1) * rpt, n8 - 1), col))

    const = lambda b, i: (0, 0)
    return pl.pallas_call(
        functools.partial(_hyb_out_kernel, n_chunks=tc // RET_CHUNK, nsteps=nsteps),
        grid=(batch, nsteps),
        in_specs=[blk(0, D_MODEL),
                  blk(0), blk(1), blk(2), blk(3), blk(4), blk(5), blk(6), blk(0),
                  halo_prev(1), halo_prev(2), halo_next(1), halo_next(2),
                  pl.BlockSpec((1, RET_HEADS), const),
                  pl.BlockSpec((3, CONV_CH), const),
                  pl.BlockSpec((1, RET_WIDTH), const),
                  pl.BlockSpec((CONV_CH + RET_WIDTH, D_MODEL), const)],
        out_specs=blk(0, D_MODEL),
        out_shape=jax.ShapeDtypeStruct((t, D_MODEL), F32),
        scratch_shapes=[pltpu.VMEM((RET_HEADS, RET_DIM, RET_DIM), F32),
                        pltpu.VMEM((tc, CONV_CH + RET_WIDTH), BF16)],
        compiler_params=pltpu.CompilerParams(dimension_semantics=("arbitrary", "arbitrary"),
                                             vmem_limit_bytes=VMEM_LIMIT),
        name="hyb_out",
    )(x2, proj, proj, proj, proj, proj, proj, proj, o_f, proj, proj, proj, proj, dec, conv_w, gn_w, w_out)


def _ffn_kernel(*refs, pre_proj, final_norm):
    refs = list(refs)
    x_ref = refs.pop(0)
    if pre_proj:
        a_ref, wo_ref = refs.pop(0), refs.pop(0)
    nw_ref, wg_ref, wu_ref, wd_ref = refs[:4]
    refs = refs[4:]
    if final_norm:
        fw_ref = refs.pop(0)
    o_ref = refs.pop(0)

    x = x_ref[...]
    if pre_proj:
        x = x + jnp.dot(a_ref[...], wo_ref[...], preferred_element_type=F32)
    xn = _rms(x, nw_ref[...], NORM_EPS).astype(BF16)
    acc = x
    for c in range(D_FF // FF_CHUNK):
        cs = slice(c * FF_CHUNK, (c + 1) * FF_CHUNK)
        g = jnp.dot(xn, wg_ref[:, cs], preferred_element_type=F32)
        u = jnp.dot(xn, wu_ref[:, cs], preferred_element_type=F32)
        h = (jax.nn.silu(g) * u).astype(BF16)
        acc = acc + jnp.dot(h, wd_ref[cs, :], preferred_element_type=F32)
    if final_norm:
        acc = _rms(acc, fw_ref[...], NORM_EPS)
    o_ref[...] = acc


def _ffn(x2, nw, wg, wu, wd, tm, pre=None, final_w=None):
    t = x2.shape[0]
    row = lambda i: (i, 0)
    const = lambda i: (0, 0)
    single = pl.Buffered(1)
    args = [x2]
    specs = [pl.BlockSpec((tm, D_MODEL), row)]
    if pre is not None:
        a, wo = pre
        args += [a, wo]
        specs += [pl.BlockSpec((tm, D_MODEL), row),
                  pl.BlockSpec((D_MODEL, D_MODEL), const, pipeline_mode=single)]
    args += [nw, wg, wu, wd]
    specs += [pl.BlockSpec((1, D_MODEL), const),
              pl.BlockSpec((D_MODEL, D_FF), const, pipeline_mode=single),
              pl.BlockSpec((D_MODEL, D_FF), const, pipeline_mode=single),
              pl.BlockSpec((D_FF, D_MODEL), const, pipeline_mode=single)]
    if final_w is not None:
        args.append(final_w)
        specs.append(pl.BlockSpec((1, D_MODEL), const))
    return pl.pallas_call(
        functools.partial(_ffn_kernel, pre_proj=pre is not None, final_norm=final_w is not None),
        grid=(t // tm,),
        in_specs=specs,
        out_specs=pl.BlockSpec((tm, D_MODEL), row),
        out_shape=jax.ShapeDtypeStruct((t, D_MODEL), F32),
        compiler_params=pltpu.CompilerParams(dimension_semantics=("arbitrary",),
                                             vmem_limit_bytes=VMEM_LIMIT),
        name="ffn",
    )(*args)


def _qkv_kernel(x_ref, nw_ref, w_ref, cos_ref, slo_ref, shi_ref, qt_ref, k_ref, vt_ref):
    xn = _rms(x_ref[...], nw_ref[...], NORM_EPS).astype(BF16)
    cos, slo, shi = cos_ref[...], slo_ref[...], shi_ref[...]
    n_qk = D_MODEL // COL_CHUNK
    for c in range(3 * n_qk):
        y = jnp.dot(xn, w_ref[:, c * COL_CHUNK:(c + 1) * COL_CHUNK], preferred_element_type=F32)
        kind, cc = divmod(c, n_qk)
        for g in range(COL_CHUNK // LANES):
            yg = y[:, g * LANES:(g + 1) * LANES]
            lo = cc * COL_CHUNK + g * LANES
            if kind == 0:
                r = _rope128(yg, cos, slo, shi, DIFF_DIM // 2) * (DIFF_DIM ** -0.5 * LOG2E)
                qt_ref[0, lo:lo + LANES, :] = r.T.astype(BF16)
            elif kind == 1:
                k_ref[:, lo:lo + LANES] = _rope128(yg, cos, slo, shi, DIFF_DIM // 2).astype(BF16)
            else:
                vt_ref[0, 0, lo:lo + LANES, :] = yg.T.astype(BF16)


def _qkv(x2, nw, w_qkv, tabs, batch, seq, tm):
    t = x2.shape[0]
    spt = seq // tm
    row = lambda i: (i, 0)
    pos = lambda i: (i % spt, 0)
    const = lambda i: (0, 0)
    return pl.pallas_call(
        _qkv_kernel,
        grid=(t // tm,),
        in_specs=[pl.BlockSpec((tm, D_MODEL), row),
                  pl.BlockSpec((1, D_MODEL), const),
                  pl.BlockSpec((D_MODEL, 3 * D_MODEL), const),
                  pl.BlockSpec((tm, LANES), pos),
                  pl.BlockSpec((tm, LANES), pos),
                  pl.BlockSpec((tm, LANES), pos)],
        out_specs=[pl.BlockSpec((1, D_MODEL, tm), lambda i: (i // spt, 0, i % spt)),
                   pl.BlockSpec((tm, D_MODEL), row),
                   pl.BlockSpec((1, 1, D_MODEL, tm), lambda i: (i // spt, i % spt, 0, 0))],
        out_shape=[jax.ShapeDtypeStruct((batch, D_MODEL, seq), BF16),
                   jax.ShapeDtypeStruct((t, D_MODEL), BF16),
                   jax.ShapeDtypeStruct((batch, spt, D_MODEL, tm), BF16)],
        compiler_params=pltpu.CompilerParams(dimension_semantics=("arbitrary",),
                                             vmem_limit_bytes=VMEM_LIMIT),
        name="qkv",
    )(x2, nw, w_qkv, *tabs)


ONES_ROWS = 16


def _attn_kernel(qt_ref, k_ref, vt_ref, lq1_ref, lk1_ref, lq2_ref, lk2_ref, sub_ref, o_ref,
                 qz_sc, s_a, s_b, mx_a, mx_b, m_sc, acc_sc, *, tk, nk, lambda_init):
    hd = 2 * DIFF_DIM
    qt = qt_ref[0]
    rowi = lax.broadcasted_iota(jnp.int32, qt.shape, 0)
    zero = jnp.zeros_like(qt)
    qz_sc[0] = jnp.where(rowi < DIFF_DIM, qt, zero)
    qz_sc[1] = jnp.where(rowi >= DIFF_DIM, qt, zero)
    m_sc[...] = jnp.full_like(m_sc, -jnp.inf)
    acc_sc[...] = jnp.zeros_like(acc_sc)
    ones = jnp.ones((ONES_ROWS, tk), BF16)

    def scores(j, s_ref, mx_ref):
        off = pl.multiple_of(j * tk, tk)
        k = k_ref[pl.ds(off, tk), :]
        for idx in range(2):
            st = jnp.dot(k, qz_sc[idx], preferred_element_type=F32).astype(BF16)
            s_ref[idx] = st
            mx_ref[idx] = jnp.max(st, axis=0, keepdims=True).astype(F32)

    def accumulate(j, s_ref, mx_ref):
        vt = jnp.concatenate([vt_ref[0, j], ones], axis=0)
        for idx in range(2):
            m_prev = m_sc[idx]
            m_new = jnp.maximum(m_prev, mx_ref[idx])
            alpha = jnp.exp2(m_prev - m_new)
            p = jnp.exp2(s_ref[idx] - m_new.astype(BF16))
            acc_sc[idx] = acc_sc[idx] * alpha + jnp.dot(vt, p, preferred_element_type=F32)
            m_sc[idx] = m_new

    scores(0, s_a, mx_a)

    def body(i, carry):
        scores(2 * i + 1, s_b, mx_b)
        accumulate(2 * i, s_a, mx_a)
        scores(2 * i + 2, s_a, mx_a)
        accumulate(2 * i + 1, s_b, mx_b)
        return carry

    lax.fori_loop(0, nk // 2 - 1, body, 0)
    scores(nk - 1, s_b, mx_b)
    accumulate(nk - 2, s_a, mx_a)
    accumulate(nk - 1, s_b, mx_b)

    lam = (jnp.exp(jnp.sum(lq1_ref[...] * lk1_ref[...], axis=-1, keepdims=True))
           - jnp.exp(jnp.sum(lq2_ref[...] * lk2_ref[...], axis=-1, keepdims=True)) + lambda_init)
    o = (acc_sc[0, 0:hd, :] / acc_sc[0, hd:hd + 1, :]
         - lam * (acc_sc[1, 0:hd, :] / acc_sc[1, hd:hd + 1, :]))
    ms = jnp.mean(o * o, axis=0, keepdims=True)
    o = (o * lax.rsqrt(ms + SUBLN_EPS)).T
    o_ref[...] = (o * sub_ref[...] * (1.0 - lambda_init)).astype(BF16)


def _attn(qt, k2, vt, lq1, lk1, lq2, lk2, subln, batch, seq, tq, tk, lambda_init):
    nq = seq // tq
    nk = seq // tk
    assert nk % 2 == 0, "the skewed key loop handles key chunks in pairs"
    hd = 2 * DIFF_DIM
    const = lambda b, h, i: (0, 0)
    return pl.pallas_call(
        functools.partial(_attn_kernel, tk=tk, nk=nk, lambda_init=lambda_init),
        grid=(batch, DIFF_HEADS, nq),
        in_specs=[pl.BlockSpec((1, hd, tq), lambda b, h, i: (b, h, i)),
                  pl.BlockSpec((seq, hd), lambda b, h, i: (b, h)),
                  pl.BlockSpec((1, nk, hd, tk), lambda b, h, i: (b, 0, h, 0)),
                  pl.BlockSpec((1, DIFF_DIM), const), pl.BlockSpec((1, DIFF_DIM), const),
                  pl.BlockSpec((1, DIFF_DIM), const), pl.BlockSpec((1, DIFF_DIM), const),
                  pl.BlockSpec((1, hd), const)],
        out_specs=pl.BlockSpec((tq, hd), lambda b, h, i: (b * nq + i, h)),
        out_shape=jax.ShapeDtypeStruct((batch * seq, DIFF_HEADS * hd), BF16),
        scratch_shapes=[pltpu.VMEM((2, hd, tq), BF16),
                        pltpu.VMEM((2, tk, tq), BF16), pltpu.VMEM((2, tk, tq), BF16),
                        pltpu.VMEM((2, 1, tq), F32), pltpu.VMEM((2, 1, tq), F32),
                        pltpu.VMEM((2, 1, tq), F32),
                        pltpu.VMEM((2, hd + ONES_ROWS, tq), F32)],
        compiler_params=pltpu.CompilerParams(dimension_semantics=("arbitrary", "arbitrary", "arbitrary"),
                                             vmem_limit_bytes=VMEM_LIMIT),
        name="attn",
    )(qt, k2, vt, lq1, lk1, lq2, lk2, subln)


def _trunk(x, p):
    batch, seq, _ = x.shape
    t = batch * seq
    tm = min(ROW_TILE, seq)
    tq = min(ATT_TQ, seq)
    x2 = x.reshape(t, D_MODEL)

    proj = _hyb_in(x2, p["norm_mix"][0:1], p["hyb_w_in"], _rope_tables(seq, RET_DIM), seq, tm)
    o_f = _ret_fwd(proj, p["hyb_decay_fwd"], batch, seq, tm)
    x2 = _hyb_out(x2, proj, o_f, p["hyb_decay_bwd"], p["hyb_conv_w"], p["hyb_gn"], p["hyb_w_out"],
                  batch, seq, tm)
    x2 = _ffn(x2, p["norm_ffn"][0:1], p["ffn_w_gate"][0], p["ffn_w_up"][0], p["ffn_w_down"][0], tm)

    lambda_init = 0.8 - 0.6 * math.exp(-0.3 * 1)
    qt, k2, vt = _qkv(x2, p["norm_mix"][1:2], p["diff_w_qkv"], _rope_tables(seq, DIFF_DIM), batch, seq, tm)
    att = _attn(qt, k2, vt, p["diff_lq1"], p["diff_lk1"], p["diff_lq2"], p["diff_lk2"], p["diff_subln"],
                batch, seq, tq, tm, lambda_init)
    x2 = _ffn(x2, p["norm_ffn"][1:2], p["ffn_w_gate"][1], p["ffn_w_up"][1], p["ffn_w_down"][1], tm,
              pre=(att, p["diff_w_out"]), final_w=p["norm_final"])
    return x2.reshape(batch, seq, D_MODEL)


def kernel(x_prompt, x_sample, norm_mix, norm_ffn, norm_final, hyb_w_in, hyb_conv_w, hyb_decay_fwd,
           hyb_decay_bwd, hyb_gn, hyb_w_out, diff_w_qkv, diff_lq1, diff_lk1, diff_lq2, diff_lk2,
           diff_subln, diff_w_out, ffn_w_gate, ffn_w_up, ffn_w_down):
    assert norm_mix.shape[0] == 2 and hyb_w_in.shape[0] == 1 and diff_w_qkv.shape[0] == 1
    p = {
        "norm_mix": norm_mix, "norm_ffn": norm_ffn, "norm_final": norm_final.reshape(1, D_MODEL),
        "hyb_w_in": hyb_w_in[0].astype(BF16), "hyb_conv_w": hyb_conv_w[0],
        "hyb_decay_fwd": hyb_decay_fwd, "hyb_decay_bwd": hyb_decay_bwd,
        "hyb_gn": hyb_gn, "hyb_w_out": hyb_w_out[0].astype(BF16),
        "diff_w_qkv": diff_w_qkv[0].astype(BF16),
        "diff_lq1": diff_lq1, "diff_lk1": diff_lk1, "diff_lq2": diff_lq2, "diff_lk2": diff_lk2,
        "diff_subln": diff_subln, "diff_w_out": diff_w_out[0].astype(BF16),
        "ffn_w_gate": ffn_w_gate.astype(BF16), "ffn_w_up": ffn_w_up.astype(BF16),
        "ffn_w_down": ffn_w_down.astype(BF16),
    }
    return (_trunk(x_prompt, p), _trunk(x_sample, p))
```

```python
import functools
import math

import jax
import jax.numpy as jnp
from jax import lax
from jax.experimental import pallas as pl
from jax.experimental.pallas import tpu as pltpu

F32 = jnp.float32
BF16 = jnp.bfloat16

D_MODEL = 1024
CONV_CH = 512
RET_HEADS = 4
RET_DIM = 128
RET_WIDTH = RET_HEADS * RET_DIM
RET_CHUNK = 128
DIFF_HEADS = 8
DIFF_DIM = 64
D_FF = 2816
ROPE_THETA = 10000.0
NORM_EPS = 1e-6
LOG2E = math.log2(math.e)
GN_EPS = 1e-5
SUBLN_EPS = 1e-5
IN_PROJ_COLS = 3 * CONV_CH + 4 * RET_WIDTH

LANES = 128
SUBLANES = 8
COL_CHUNK = 512
MXU_COLS = 256
FF_CHUNK = MXU_COLS
ROW_TILE = 512
ATT_TQ = 2048
ATT_TK = 512
ATT_UNROLL = 4
VMEM_LIMIT = 56 * 1024 * 1024


def _rms(x, w, eps):
    ms = jnp.mean(x * x, axis=-1, keepdims=True)
    return (x * lax.rsqrt(ms + eps)) * w


def _rope_tables(seq, dim):
    half = dim // 2
    inv = ROPE_THETA ** (-jnp.arange(0, dim, 2, dtype=F32) / dim)
    ang = jnp.arange(seq, dtype=F32)[:, None] * inv[None, :]
    cos = jnp.cos(ang)
    sin = jnp.sin(ang)
    zero = jnp.zeros_like(sin)
    reps = LANES // dim
    cos_t = jnp.tile(jnp.concatenate([cos, cos], axis=1), (1, reps))
    sin_lo = jnp.tile(jnp.concatenate([-sin, zero], axis=1), (1, reps))
    sin_hi = jnp.tile(jnp.concatenate([zero, sin], axis=1), (1, reps))
    return cos_t, sin_lo, sin_hi


def _rope128(y, cos, sin_lo, sin_hi, half):
    return (y * cos + pltpu.roll(y, LANES - half, 1) * sin_lo + pltpu.roll(y, half, 1) * sin_hi)


def _hyb_in_kernel(x_ref, nw_ref, w_ref, cos_ref, slo_ref, shi_ref, o_ref):
    xn = _rms(x_ref[...], nw_ref[...], NORM_EPS).astype(BF16)
    cos, slo, shi = cos_ref[...], slo_ref[...], shi_ref[...]
    q_chunk = 3 * CONV_CH // COL_CHUNK
    for c in range(IN_PROJ_COLS // COL_CHUNK):
        y = jnp.dot(xn, w_ref[:, c * COL_CHUNK:(c + 1) * COL_CHUNK], preferred_element_type=F32)
        if c in (q_chunk, q_chunk + 1):
            scale = 1.0 if c == q_chunk else RET_DIM ** -0.5
            for g in range(COL_CHUNK // LANES):
                r = _rope128(y[:, g * LANES:(g + 1) * LANES], cos, slo, shi, RET_DIM // 2)
                if scale != 1.0:
                    r = r * scale
                o_ref[:, c * COL_CHUNK + g * LANES:c * COL_CHUNK + (g + 1) * LANES] = r.astype(BF16)
        else:
            o_ref[:, c * COL_CHUNK:(c + 1) * COL_CHUNK] = y.astype(BF16)


def _hyb_in(x2, nw, w_in, tabs, seq, tm):
    t = x2.shape[0]
    spt = seq // tm
    row = lambda i: (i, 0)
    pos = lambda i: (i % spt, 0)
    const = lambda i: (0, 0)
    return pl.pallas_call(
        _hyb_in_kernel,
        grid=(t // tm,),
        in_specs=[pl.BlockSpec((tm, D_MODEL), row),
                  pl.BlockSpec((1, D_MODEL), const),
                  pl.BlockSpec((D_MODEL, IN_PROJ_COLS), const),
                  pl.BlockSpec((tm, LANES), pos),
                  pl.BlockSpec((tm, LANES), pos),
                  pl.BlockSpec((tm, LANES), pos)],
        out_specs=pl.BlockSpec((tm, IN_PROJ_COLS), row),
        out_shape=jax.ShapeDtypeStruct((t, IN_PROJ_COLS), BF16),
        compiler_params=pltpu.CompilerParams(dimension_semantics=("arbitrary",),
                                             vmem_limit_bytes=VMEM_LIMIT),
        name="hyb_in",
    )(x2, nw, w_in, *tabs)


def _decay_tables(lg, strict):
    c = RET_CHUNK
    ii = lax.broadcasted_iota(jnp.int32, (c, c), 0).astype(F32)
    jj = lax.broadcasted_iota(jnp.int32, (c, c), 1).astype(F32)
    if strict:
        rel = jj - ii
        mask = rel > 0
        kdec = jnp.exp(lg * ii)
        qdec = jnp.exp(lg * (c - ii))
    else:
        rel = ii - jj
        mask = rel >= 0
        kdec = jnp.exp(lg * (c - 1.0 - ii))
        qdec = jnp.exp(lg * (ii + 1.0))
    dintra = jnp.where(mask, jnp.exp(lg * jnp.maximum(rel, 0.0)), 0.0)
    return dintra, kdec, qdec, jnp.exp(lg * c)


def _ret_chunk(q, k, v, st, dintra, kdec, qdec, cdec):
    sc = lax.dot_general(q, k, (((1,), (1,)), ((), ())), preferred_element_type=F32) * dintra
    o = jnp.dot(sc.astype(BF16), v, preferred_element_type=F32)
    o = o + jnp.dot(q, st.astype(BF16), preferred_element_type=F32) * qdec
    kd_t = (k.astype(F32) * kdec).T.astype(BF16)
    kv = jnp.dot(kd_t, v, preferred_element_type=F32)
    return o, st * cdec + kv


def _ret_fwd_kernel(q_ref, k_ref, v_ref, dec_ref, of_ref, state_sc, *, n_chunks):
    @pl.when(pl.program_id(1) == 0)
    def _():
        state_sc[...] = jnp.zeros_like(state_sc)

    for h in range(RET_HEADS):
        lg = -jnp.exp(dec_ref[:, h:h + 1])
        tabs = _decay_tables(lg, strict=False)
        cols = slice(h * RET_DIM, (h + 1) * RET_DIM)
        for c in range(n_chunks):
            rows = slice(c * RET_CHUNK, (c + 1) * RET_CHUNK)
            o, st = _ret_chunk(q_ref[rows, cols], k_ref[rows, cols], v_ref[rows, cols], state_sc[h], *tabs)
            of_ref[rows, cols] = o
            state_sc[h] = st


def _ret_fwd(proj, dec, batch, seq, tc):
    t = proj.shape[0]
    nsteps = seq // tc
    qb = 3 * CONV_CH // RET_WIDTH

    def blk(col):
        return pl.BlockSpec((tc, RET_WIDTH), lambda b, i: (b * nsteps + i, col))

    return pl.pallas_call(
        functools.partial(_ret_fwd_kernel, n_chunks=tc // RET_CHUNK),
        grid=(batch, nsteps),
        in_specs=[blk(qb), blk(qb + 1), blk(qb + 2), pl.BlockSpec((1, RET_HEADS), lambda b, i: (0, 0))],
        out_specs=pl.BlockSpec((tc, RET_WIDTH), lambda b, i: (b * nsteps + i, 0)),
        out_shape=jax.ShapeDtypeStruct((t, RET_WIDTH), F32),
        scratch_shapes=[pltpu.VMEM((RET_HEADS, RET_DIM, RET_DIM), F32)],
        compiler_params=pltpu.CompilerParams(dimension_semantics=("arbitrary", "arbitrary"),
                                             vmem_limit_bytes=VMEM_LIMIT),
        name="ret_fwd",
    )(proj, proj, proj, dec)


def _hyb_out_kernel(x_ref, ab_ref, ac_ref, ah_ref, q_ref, k_ref, v_ref, g_ref, of_ref,
                    acp_ref, ahp_ref, acn_ref, ahn_ref, dec_ref, cw_ref, gn_ref, wo_ref,
                    o_ref, state_sc, y_sc, *, n_chunks, nsteps):
    step = pl.program_id(1)

    @pl.when(step == 0)
    def _():
        state_sc[...] = jnp.zeros_like(state_sc)

    for h in range(RET_HEADS):
        lg = -jnp.exp(dec_ref[:, h:h + 1])
        tabs = _decay_tables(lg, strict=True)
        cols = slice(h * RET_DIM, (h + 1) * RET_DIM)
        gn_w = gn_ref[:, cols]
        for c in reversed(range(n_chunks)):
            rows = slice(c * RET_CHUNK, (c + 1) * RET_CHUNK)
            o, st = _ret_chunk(q_ref[rows, cols], k_ref[rows, cols], v_ref[rows, cols], state_sc[h], *tabs)
            state_sc[h] = st
            o = o + of_ref[rows, cols]
            mu = jnp.mean(o, axis=-1, keepdims=True)
            d = o - mu
            var = jnp.mean(d * d, axis=-1, keepdims=True)
            on = d * lax.rsqrt(var + GN_EPS) * gn_w
            g = g_ref[rows, cols].astype(F32)
            y_sc[rows, CONV_CH + h * RET_DIM:CONV_CH + (h + 1) * RET_DIM] = (jax.nn.silu(g) * on).astype(BF16)

    tc = n_chunks * RET_CHUNK
    tile = nsteps - 1 - step
    u = ac_ref[...].astype(F32) * ah_ref[...].astype(F32)
    u_prev = acp_ref[SUBLANES - 1:SUBLANES, :].astype(F32) * ahp_ref[SUBLANES - 1:SUBLANES, :].astype(F32)
    u_next = acn_ref[0:1, :].astype(F32) * ahn_ref[0:1, :].astype(F32)
    u_prev = jnp.where(tile > 0, u_prev, 0.0)
    u_next = jnp.where(tile < nsteps - 1, u_next, 0.0)
    ridx = lax.broadcasted_iota(jnp.int32, u.shape, 0)
    up = jnp.where(ridx == 0, u_prev, pltpu.roll(u, 1, 0))
    un = jnp.where(ridx == tc - 1, u_next, pltpu.roll(u, tc - 1, 0))
    conv = cw_ref[0:1, :] * up + cw_ref[1:2, :] * u + cw_ref[2:3, :] * un
    y_sc[:, 0:CONV_CH] = (ab_ref[...].astype(F32) * conv).astype(BF16)

    o_ref[...] = x_ref[...] + jnp.dot(y_sc[...], wo_ref[...], preferred_element_type=F32)


def _hyb_out(x2, proj, o_f, dec, conv_w, gn_w, w_out, batch, seq, tc):
    t = x2.shape[0]
    nsteps = seq // tc
    rpt = tc // SUBLANES
    n8 = t // SUBLANES
    tile = lambda b, i: b * nsteps + (nsteps - 1 - i)

    def blk(col, width=RET_WIDTH):
        return pl.BlockSpec((tc, width), lambda b, i: (tile(b, i), col))

    def halo_prev(col):
        return pl.BlockSpec((SUBLANES, CONV_CH), lambda b, i: (jnp.maximum(tile(b, i) * rpt - 1, 0), col))

    def halo_next(col):
        return pl.BlockSpec((SUBLANES, CONV_CH), lambda b, i: (jnp.minimum((tile(b, i) + 1) * rpt, n8 - 1), col))

    const = lambda b, i: (0, 0)
    return pl.pallas_call(
        functools.partial(_hyb_out_kernel, n_chunks=tc // RET_CHUNK, nsteps=nsteps),
        grid=(batch, nsteps),
        in_specs=[blk(0, D_MODEL),
                  blk(0), blk(1), blk(2), blk(3), blk(4), blk(5), blk(6), blk(0),
                  halo_prev(1), halo_prev(2), halo_next(1), halo_next(2),
                  pl.BlockSpec((1, RET_HEADS), const),
                  pl.BlockSpec((3, CONV_CH), const),
                  pl.BlockSpec((1, RET_WIDTH), const),
                  pl.BlockSpec((CONV_CH + RET_WIDTH, D_MODEL), const)],
        out_specs=blk(0, D_MODEL),
        out_shape=jax.ShapeDtypeStruct((t, D_MODEL), F32),
        scratch_shapes=[pltpu.VMEM((RET_HEADS, RET_DIM, RET_DIM), F32),
                        pltpu.VMEM((tc, CONV_CH + RET_WIDTH), BF16)],
        compiler_params=pltpu.CompilerParams(dimension_semantics=("arbitrary", "arbitrary"),
                                             vmem_limit_bytes=VMEM_LIMIT),
        name="hyb_out",
    )(x2, proj, proj, proj, proj, proj, proj, proj, o_f, proj, proj, proj, proj, dec, conv_w, gn_w, w_out)


def _ffn_kernel(*refs, pre_proj, final_norm):
    refs = list(refs)
    x_ref = refs.pop(0)
    if pre_proj:
        a_ref, wo_ref = refs.pop(0), refs.pop(0)
    nw_ref, wg_ref, wu_ref, wd_ref = refs[:4]
    refs = refs[4:]
    if final_norm:
        fw_ref = refs.pop(0)
    o_ref = refs.pop(0)

    x = x_ref[...]
    if pre_proj:
        x = x + jnp.dot(a_ref[...], wo_ref[...], preferred_element_type=F32)
    xn = _rms(x, nw_ref[...], NORM_EPS).astype(BF16)
    acc = x
    for c in range(D_FF // FF_CHUNK):
        cs = slice(c * FF_CHUNK, (c + 1) * FF_CHUNK)
        g = jnp.dot(xn, wg_ref[:, cs], preferred_element_type=F32)
        u = jnp.dot(xn, wu_ref[:, cs], preferred_element_type=F32)
        h = (jax.nn.silu(g) * u).astype(BF16)
        acc = acc + jnp.dot(h, wd_ref[cs, :], preferred_element_type=F32)
    if final_norm:
        acc = _rms(acc, fw_ref[...], NORM_EPS)
    o_ref[...] = acc


def _ffn(x2, nw, wg, wu, wd, tm, pre=None, final_w=None):
    t = x2.shape[0]
    row = lambda i: (i, 0)
    const = lambda i: (0, 0)
    single = pl.Buffered(1)
    args = [x2]
    specs = [pl.BlockSpec((tm, D_MODEL), row)]
    if pre is not None:
        a, wo = pre
        args += [a, wo]
        specs += [pl.BlockSpec((tm, D_MODEL), row),
                  pl.BlockSpec((D_MODEL, D_MODEL), const, pipeline_mode=single)]
    args += [nw, wg, wu, wd]
    specs += [pl.BlockSpec((1, D_MODEL), const),
              pl.BlockSpec((D_MODEL, D_FF), const, pipeline_mode=single),
              pl.BlockSpec((D_MODEL, D_FF), const, pipeline_mode=single),
              pl.BlockSpec((D_FF, D_MODEL), const, pipeline_mode=single)]
    if final_w is not None:
        args.append(final_w)
        specs.append(pl.BlockSpec((1, D_MODEL), const))
    return pl.pallas_call(
        functools.partial(_ffn_kernel, pre_proj=pre is not None, final_norm=final_w is not None),
        grid=(t // tm,),
        in_specs=specs,
        out_specs=pl.BlockSpec((tm, D_MODEL), row),
        out_shape=jax.ShapeDtypeStruct((t, D_MODEL), F32),
        compiler_params=pltpu.CompilerParams(dimension_semantics=("arbitrary",),
                                             vmem_limit_bytes=VMEM_LIMIT),
        name="ffn",
    )(*args)


def _qkv_kernel(x_ref, nw_ref, w_ref, cos_ref, slo_ref, shi_ref, qt_ref, k_ref, vt_ref):
    xn = _rms(x_ref[...], nw_ref[...], NORM_EPS).astype(BF16)
    cos, slo, shi = cos_ref[...], slo_ref[...], shi_ref[...]
    n_qk = D_MODEL // COL_CHUNK
    for c in range(3 * n_qk):
        y = jnp.dot(xn, w_ref[:, c * COL_CHUNK:(c + 1) * COL_CHUNK], preferred_element_type=F32)
        kind, cc = divmod(c, n_qk)
        for g in range(COL_CHUNK // LANES):
            yg = y[:, g * LANES:(g + 1) * LANES]
            lo = cc * COL_CHUNK + g * LANES
            if kind == 0:
                r = _rope128(yg, cos, slo, shi, DIFF_DIM // 2) * (DIFF_DIM ** -0.5 * LOG2E)
                qt_ref[0, lo:lo + LANES, :] = r.T.astype(BF16)
            elif kind == 1:
                k_ref[:, lo:lo + LANES] = _rope128(yg, cos, slo, shi, DIFF_DIM // 2).astype(BF16)
            else:
                vt_ref[0, 0, lo:lo + LANES, :] = yg.T.astype(BF16)


def _qkv(x2, nw, w_qkv, tabs, batch, seq, tm):
    t = x2.shape[0]
    spt = seq // tm
    row = lambda i: (i, 0)
    pos = lambda i: (i % spt, 0)
    const = lambda i: (0, 0)
    return pl.pallas_call(
        _qkv_kernel,
        grid=(t // tm,),
        in_specs=[pl.BlockSpec((tm, D_MODEL), row),
                  pl.BlockSpec((1, D_MODEL), const),
                  pl.BlockSpec((D_MODEL, 3 * D_MODEL), const),
                  pl.BlockSpec((tm, LANES), pos),
                  pl.BlockSpec((tm, LANES), pos),
                  pl.BlockSpec((tm, LANES), pos)],
        out_specs=[pl.BlockSpec((1, D_MODEL, tm), lambda i: (i // spt, 0, i % spt)),
                   pl.BlockSpec((tm, D_MODEL), row),
                   pl.BlockSpec((1, 1, D_MODEL, tm), lambda i: (i // spt, i % spt, 0, 0))],
        out_shape=[jax.ShapeDtypeStruct((batch, D_MODEL, seq), BF16),
                   jax.ShapeDtypeStruct((t, D_MODEL), BF16),
                   jax.ShapeDtypeStruct((batch, spt, D_MODEL, tm), BF16)],
        compiler_params=pltpu.CompilerParams(dimension_semantics=("arbitrary",),
                                             vmem_limit_bytes=VMEM_LIMIT),
        name="qkv",
    )(x2, nw, w_qkv, *tabs)


ONES_ROWS = 16


def _attn_kernel(qt_ref, k_ref, vt_ref, lq1_ref, lk1_ref, lq2_ref, lk2_ref, sub_ref, o_ref,
                 qz_sc, s_a, s_b, mx_a, mx_b, m_sc, acc_sc, *, tk, nk, lambda_init):
    hd = 2 * DIFF_DIM
    qt = qt_ref[0]
    rowi = lax.broadcasted_iota(jnp.int32, qt.shape, 0)
    zero = jnp.zeros_like(qt)
    qz_sc[0] = jnp.where(rowi < DIFF_DIM, qt, zero)
    qz_sc[1] = jnp.where(rowi >= DIFF_DIM, qt, zero)
    m_sc[...] = jnp.full_like(m_sc, -jnp.inf)
    acc_sc[...] = jnp.zeros_like(acc_sc)
    ones = jnp.ones((ONES_ROWS, tk), BF16)

    subs = [(idx, slice(c, c + MXU_COLS)) for idx in range(2) for c in range(0, qt.shape[1], MXU_COLS)]

    def scores_sub(j, s_ref, mx_ref, idx, cs):
        off = pl.multiple_of(j * tk, tk)
        k = k_ref[pl.ds(off, tk), :]
        st = jnp.dot(k, qz_sc[idx, :, cs], preferred_element_type=F32).astype(BF16)
        s_ref[idx, :, cs] = st
        mx_ref[idx, :, cs] = jnp.max(st, axis=0, keepdims=True).astype(F32)

    def accumulate_sub(j, s_ref, mx_ref, idx, cs):
        vt = jnp.concatenate([vt_ref[0, j], ones], axis=0)
        m_prev = m_sc[idx, :, cs]
        m_new = jnp.maximum(m_prev, mx_ref[idx, :, cs])
        alpha = jnp.exp2(m_prev - m_new)
        p = jnp.exp2(s_ref[idx, :, cs] - m_new.astype(BF16))
        acc_sc[idx, :, cs] = acc_sc[idx, :, cs] * alpha + jnp.dot(vt, p, preferred_element_type=F32)
        m_sc[idx, :, cs] = m_new

    bufs = ((s_a, mx_a), (s_b, mx_b))

    def step(j, parity):
        for idx, cs in subs:
            scores_sub(j + 1, *bufs[1 - parity], idx, cs)
            accumulate_sub(j, *bufs[parity], idx, cs)

    for idx, cs in subs:
        scores_sub(0, *bufs[0], idx, cs)

    def body(i, carry):
        for u in range(ATT_UNROLL):
            step(ATT_UNROLL * i + u, u % 2)
        return carry

    n_loop = (nk - 1) // ATT_UNROLL
    lax.fori_loop(0, n_loop, body, 0)
    for j in range(n_loop * ATT_UNROLL, nk - 1):
        step(j, j % 2)
    for idx, cs in subs:
        accumulate_sub(nk - 1, *bufs[(nk - 1) % 2], idx, cs)

    lam = (jnp.exp(jnp.sum(lq1_ref[...] * lk1_ref[...], axis=-1, keepdims=True))
           - jnp.exp(jnp.sum(lq2_ref[...] * lk2_ref[...], axis=-1, keepdims=True)) + lambda_init)
    o = (acc_sc[0, 0:hd, :] / acc_sc[0, hd:hd + 1, :]
         - lam * (acc_sc[1, 0:hd, :] / acc_sc[1, hd:hd + 1, :]))
    ms = jnp.mean(o * o, axis=0, keepdims=True)
    o = (o * lax.rsqrt(ms + SUBLN_EPS)).T
    o_ref[...] = (o * sub_ref[...] * (1.0 - lambda_init)).astype(BF16)


def _attn(qt, k2, vt, lq1, lk1, lq2, lk2, subln, batch, seq, tq, tk, lambda_init):
    nq = seq // tq
    nk = seq // tk
    hd = 2 * DIFF_DIM
    const = lambda b, h, i: (0, 0)
    return pl.pallas_call(
        functools.partial(_attn_kernel, tk=tk, nk=nk, lambda_init=lambda_init),
        grid=(batch, DIFF_HEADS, nq),
        in_specs=[pl.BlockSpec((1, hd, tq), lambda b, h, i: (b, h, i)),
                  pl.BlockSpec((seq, hd), lambda b, h, i: (b, h)),
                  pl.BlockSpec((1, nk, hd, tk), lambda b, h, i: (b, 0, h, 0)),
                  pl.BlockSpec((1, DIFF_DIM), const), pl.BlockSpec((1, DIFF_DIM), const),
                  pl.BlockSpec((1, DIFF_DIM), const), pl.BlockSpec((1, DIFF_DIM), const),
                  pl.BlockSpec((1, hd), const)],
        out_specs=pl.BlockSpec((tq, hd), lambda b, h, i: (b * nq + i, h)),
        out_shape=jax.ShapeDtypeStruct((batch * seq, DIFF_HEADS * hd), BF16),
        scratch_shapes=[pltpu.VMEM((2, hd, tq), BF16),
                        pltpu.VMEM((2, tk, tq), BF16), pltpu.VMEM((2, tk, tq), BF16),
                        pltpu.VMEM((2, 1, tq), F32), pltpu.VMEM((2, 1, tq), F32),
                        pltpu.VMEM((2, 1, tq), F32),
                        pltpu.VMEM((2, hd + ONES_ROWS, tq), F32)],
        compiler_params=pltpu.CompilerParams(dimension_semantics=("arbitrary", "arbitrary", "arbitrary"),
                                             vmem_limit_bytes=VMEM_LIMIT),
        name="attn",
    )(qt, k2, vt, lq1, lk1, lq2, lk2, subln)


def _trunk(x, p):
    batch, seq, _ = x.shape
    t = batch * seq
    tm = min(ROW_TILE, seq)
    tq = min(ATT_TQ, seq)
    x2 = x.reshape(t, D_MODEL)

    proj = _hyb_in(x2, p["norm_mix"][0:1], p["hyb_w_in"], _rope_tables(seq, RET_DIM), seq, tm)
    o_f = _ret_fwd(proj, p["hyb_decay_fwd"], batch, seq, tm)
    x2 = _hyb_out(x2, proj, o_f, p["hyb_decay_bwd"], p["hyb_conv_w"], p["hyb_gn"], p["hyb_w_out"],
                  batch, seq, tm)
    x2 = _ffn(x2, p["norm_ffn"][0:1], p["ffn_w_gate"][0], p["ffn_w_up"][0], p["ffn_w_down"][0], tm)

    lambda_init = 0.8 - 0.6 * math.exp(-0.3 * 1)
    qt, k2, vt = _qkv(x2, p["norm_mix"][1:2], p["diff_w_qkv"], _rope_tables(seq, DIFF_DIM), batch, seq, tm)
    att = _attn(qt, k2, vt, p["diff_lq1"], p["diff_lk1"], p["diff_lq2"], p["diff_lk2"], p["diff_subln"],
                batch, seq, tq, tm, lambda_init)
    x2 = _ffn(x2, p["norm_ffn"][1:2], p["ffn_w_gate"][1], p["ffn_w_up"][1], p["ffn_w_down"][1], tm,
              pre=(att, p["diff_w_out"]), final_w=p["norm_final"])
    return x2.reshape(batch, seq, D_MODEL)


def kernel(x_prompt, x_sample, norm_mix, norm_ffn, norm_final, hyb_w_in, hyb_conv_w, hyb_decay_fwd,
           hyb_decay_bwd, hyb_gn, hyb_w_out, diff_w_qkv, diff_lq1, diff_lk1, diff_lq2, diff_lk2,
           diff_subln, diff_w_out, ffn_w_gate, ffn_w_up, ffn_w_down):
    assert norm_mix.shape[0] == 2 and hyb_w_in.shape[0] == 1 and diff_w_qkv.shape[0] == 1
    p = {
        "norm_mix": norm_mix, "norm_ffn": norm_ffn, "norm_final": norm_final.reshape(1, D_MODEL),
        "hyb_w_in": hyb_w_in[0].astype(BF16), "hyb_conv_w": hyb_conv_w[0],
        "hyb_decay_fwd": hyb_decay_fwd, "hyb_decay_bwd": hyb_decay_bwd,
        "hyb_gn": hyb_gn, "hyb_w_out": hyb_w_out[0].astype(BF16),
        "diff_w_qkv": diff_w_qkv[0].astype(BF16),
        "diff_lq1": diff_lq1, "diff_lk1": diff_lk1, "diff_lq2": diff_lq2, "diff_lk2": diff_lk2,
        "diff_subln": diff_subln, "diff_w_out": diff_w_out[0].astype(BF16),
        "ffn_w_gate": ffn_w_gate.astype(BF16), "ffn_w_up": ffn_w_up.astype(BF16),
        "ffn_w_down": ffn_w_down.astype(BF16),
    }
    return (_trunk(x_prompt, p), _trunk(x_sample, p))
```

```python
import functools
import math

import jax
import jax.numpy as jnp
from jax import lax
from jax.experimental import pallas as pl
from jax.experimental.pallas import tpu as pltpu

F32 = jnp.float32
BF16 = jnp.bfloat16
F8 = jnp.float8_e4m3fn

D_MODEL = 1024
CONV_CH = 512
RET_HEADS = 4
RET_DIM = 128
RET_WIDTH = RET_HEADS * RET_DIM
RET_CHUNK = 128
DIFF_HEADS = 8
DIFF_DIM = 64
D_FF = 2816
ROPE_THETA = 10000.0
NORM_EPS = 1e-6
LOG2E = math.log2(math.e)
GN_EPS = 1e-5
SUBLN_EPS = 1e-5
IN_PROJ_COLS = 3 * CONV_CH + 4 * RET_WIDTH

LANES = 128
SUBLANES = 8
COL_CHUNK = 512
MXU_COLS = 256
FF_CHUNK = MXU_COLS
ROW_TILE = 512
ATT_TQ = 2048
ATT_TK = 512
ATT_UNROLL = 4
VMEM_LIMIT = 56 * 1024 * 1024


def _rms(x, w, eps):
    ms = jnp.mean(x * x, axis=-1, keepdims=True)
    return (x * lax.rsqrt(ms + eps)) * w


def _rope_tables(seq, dim):
    half = dim // 2
    inv = ROPE_THETA ** (-jnp.arange(0, dim, 2, dtype=F32) / dim)
    ang = jnp.arange(seq, dtype=F32)[:, None] * inv[None, :]
    cos = jnp.cos(ang)
    sin = jnp.sin(ang)
    zero = jnp.zeros_like(sin)
    reps = LANES // dim
    cos_t = jnp.tile(jnp.concatenate([cos, cos], axis=1), (1, reps))
    sin_lo = jnp.tile(jnp.concatenate([-sin, zero], axis=1), (1, reps))
    sin_hi = jnp.tile(jnp.concatenate([zero, sin], axis=1), (1, reps))
    return cos_t, sin_lo, sin_hi


def _rope128(y, cos, sin_lo, sin_hi, half):
    return (y * cos + pltpu.roll(y, LANES - half, 1) * sin_lo + pltpu.roll(y, half, 1) * sin_hi)


def _hyb_in_kernel(x_ref, nw_ref, w_ref, cos_ref, slo_ref, shi_ref, o_ref):
    xn = _rms(x_ref[...], nw_ref[...], NORM_EPS).astype(BF16)
    cos, slo, shi = cos_ref[...], slo_ref[...], shi_ref[...]
    q_chunk = 3 * CONV_CH // COL_CHUNK
    for c in range(IN_PROJ_COLS // COL_CHUNK):
        y = jnp.dot(xn, w_ref[:, c * COL_CHUNK:(c + 1) * COL_CHUNK], preferred_element_type=F32)
        if c in (q_chunk, q_chunk + 1):
            scale = 1.0 if c == q_chunk else RET_DIM ** -0.5
            for g in range(COL_CHUNK // LANES):
                r = _rope128(y[:, g * LANES:(g + 1) * LANES], cos, slo, shi, RET_DIM // 2)
                if scale != 1.0:
                    r = r * scale
                o_ref[:, c * COL_CHUNK + g * LANES:c * COL_CHUNK + (g + 1) * LANES] = r.astype(BF16)
        else:
            o_ref[:, c * COL_CHUNK:(c + 1) * COL_CHUNK] = y.astype(BF16)


def _hyb_in(x2, nw, w_in, tabs, seq, tm):
    t = x2.shape[0]
    spt = seq // tm
    row = lambda i: (i, 0)
    pos = lambda i: (i % spt, 0)
    const = lambda i: (0, 0)
    return pl.pallas_call(
        _hyb_in_kernel,
        grid=(t // tm,),
        in_specs=[pl.BlockSpec((tm, D_MODEL), row),
                  pl.BlockSpec((1, D_MODEL), const),
                  pl.BlockSpec((D_MODEL, IN_PROJ_COLS), const),
                  pl.BlockSpec((tm, LANES), pos),
                  pl.BlockSpec((tm, LANES), pos),
                  pl.BlockSpec((tm, LANES), pos)],
        out_specs=pl.BlockSpec((tm, IN_PROJ_COLS), row),
        out_shape=jax.ShapeDtypeStruct((t, IN_PROJ_COLS), BF16),
        compiler_params=pltpu.CompilerParams(dimension_semantics=("arbitrary",),
                                             vmem_limit_bytes=VMEM_LIMIT),
        name="hyb_in",
    )(x2, nw, w_in, *tabs)


def _decay_tables(lg, strict):
    c = RET_CHUNK
    ii = lax.broadcasted_iota(jnp.int32, (c, c), 0).astype(F32)
    jj = lax.broadcasted_iota(jnp.int32, (c, c), 1).astype(F32)
    if strict:
        rel = jj - ii
        mask = rel > 0
        kdec = jnp.exp(lg * ii)
        qdec = jnp.exp(lg * (c - ii))
    else:
        rel = ii - jj
        mask = rel >= 0
        kdec = jnp.exp(lg * (c - 1.0 - ii))
        qdec = jnp.exp(lg * (ii + 1.0))
    dintra = jnp.where(mask, jnp.exp(lg * jnp.maximum(rel, 0.0)), 0.0)
    return dintra, kdec, qdec, jnp.exp(lg * c)


def _ret_chunk(q, k, v, st, dintra, kdec, qdec, cdec):
    sc = lax.dot_general(q, k, (((1,), (1,)), ((), ())), preferred_element_type=F32) * dintra
    o = jnp.dot(sc.astype(BF16), v, preferred_element_type=F32)
    o = o + jnp.dot(q, st.astype(BF16), preferred_element_type=F32) * qdec
    kd_t = (k.astype(F32) * kdec).T.astype(BF16)
    kv = jnp.dot(kd_t, v, preferred_element_type=F32)
    return o, st * cdec + kv


def _ret_fwd_kernel(q_ref, k_ref, v_ref, dec_ref, of_ref, state_sc, *, n_chunks):
    @pl.when(pl.program_id(1) == 0)
    def _():
        state_sc[...] = jnp.zeros_like(state_sc)

    for h in range(RET_HEADS):
        lg = -jnp.exp(dec_ref[:, h:h + 1])
        tabs = _decay_tables(lg, strict=False)
        cols = slice(h * RET_DIM, (h + 1) * RET_DIM)
        for c in range(n_chunks):
            rows = slice(c * RET_CHUNK, (c + 1) * RET_CHUNK)
            o, st = _ret_chunk(q_ref[rows, cols], k_ref[rows, cols], v_ref[rows, cols], state_sc[h], *tabs)
            of_ref[rows, cols] = o
            state_sc[h] = st


def _ret_fwd(proj, dec, batch, seq, tc):
    t = proj.shape[0]
    nsteps = seq // tc
    qb = 3 * CONV_CH // RET_WIDTH

    def blk(col):
        return pl.BlockSpec((tc, RET_WIDTH), lambda b, i: (b * nsteps + i, col))

    return pl.pallas_call(
        functools.partial(_ret_fwd_kernel, n_chunks=tc // RET_CHUNK),
        grid=(batch, nsteps),
        in_specs=[blk(qb), blk(qb + 1), blk(qb + 2), pl.BlockSpec((1, RET_HEADS), lambda b, i: (0, 0))],
        out_specs=pl.BlockSpec((tc, RET_WIDTH), lambda b, i: (b * nsteps + i, 0)),
        out_shape=jax.ShapeDtypeStruct((t, RET_WIDTH), F32),
        scratch_shapes=[pltpu.VMEM((RET_HEADS, RET_DIM, RET_DIM), F32)],
        compiler_params=pltpu.CompilerParams(dimension_semantics=("arbitrary", "arbitrary"),
                                             vmem_limit_bytes=VMEM_LIMIT),
        name="ret_fwd",
    )(proj, proj, proj, dec)


def _hyb_out_kernel(x_ref, ab_ref, ac_ref, ah_ref, q_ref, k_ref, v_ref, g_ref, of_ref,
                    acp_ref, ahp_ref, acn_ref, ahn_ref, dec_ref, cw_ref, gn_ref, wo_ref,
                    o_ref, state_sc, y_sc, *, n_chunks, nsteps):
    step = pl.program_id(1)

    @pl.when(step == 0)
    def _():
        state_sc[...] = jnp.zeros_like(state_sc)

    for h in range(RET_HEADS):
        lg = -jnp.exp(dec_ref[:, h:h + 1])
        tabs = _decay_tables(lg, strict=True)
        cols = slice(h * RET_DIM, (h + 1) * RET_DIM)
        gn_w = gn_ref[:, cols]
        for c in reversed(range(n_chunks)):
            rows = slice(c * RET_CHUNK, (c + 1) * RET_CHUNK)
            o, st = _ret_chunk(q_ref[rows, cols], k_ref[rows, cols], v_ref[rows, cols], state_sc[h], *tabs)
            state_sc[h] = st
            o = o + of_ref[rows, cols]
            mu = jnp.mean(o, axis=-1, keepdims=True)
            d = o - mu
            var = jnp.mean(d * d, axis=-1, keepdims=True)
            on = d * lax.rsqrt(var + GN_EPS) * gn_w
            g = g_ref[rows, cols].astype(F32)
            y_sc[rows, CONV_CH + h * RET_DIM:CONV_CH + (h + 1) * RET_DIM] = (jax.nn.silu(g) * on).astype(BF16)

    tc = n_chunks * RET_CHUNK
    tile = nsteps - 1 - step
    u = ac_ref[...].astype(F32) * ah_ref[...].astype(F32)
    u_prev = acp_ref[SUBLANES - 1:SUBLANES, :].astype(F32) * ahp_ref[SUBLANES - 1:SUBLANES, :].astype(F32)
    u_next = acn_ref[0:1, :].astype(F32) * ahn_ref[0:1, :].astype(F32)
    u_prev = jnp.where(tile > 0, u_prev, 0.0)
    u_next = jnp.where(tile < nsteps - 1, u_next, 0.0)
    ridx = lax.broadcasted_iota(jnp.int32, u.shape, 0)
    up = jnp.where(ridx == 0, u_prev, pltpu.roll(u, 1, 0))
    un = jnp.where(ridx == tc - 1, u_next, pltpu.roll(u, tc - 1, 0))
    conv = cw_ref[0:1, :] * up + cw_ref[1:2, :] * u + cw_ref[2:3, :] * un
    y_sc[:, 0:CONV_CH] = (ab_ref[...].astype(F32) * conv).astype(BF16)

    o_ref[...] = x_ref[...] + jnp.dot(y_sc[...], wo_ref[...], preferred_element_type=F32)


def _hyb_out(x2, proj, o_f, dec, conv_w, gn_w, w_out, batch, seq, tc):
    t = x2.shape[0]
    nsteps = seq // tc
    rpt = tc // SUBLANES
    n8 = t // SUBLANES
    tile = lambda b, i: b * nsteps + (nsteps - 1 - i)

    def blk(col, width=RET_WIDTH):
        return pl.BlockSpec((tc, width), lambda b, i: (tile(b, i), col))

    def halo_prev(col):
        return pl.BlockSpec((SUBLANES, CONV_CH), lambda b, i: (jnp.maximum(tile(b, i) * rpt - 1, 0), col))

    def halo_next(col):
        return pl.BlockSpec((SUBLANES, CONV_CH), lambda b, i: (jnp.minimum((tile(b, i) + 1) * rpt, n8 - 1), col))

    const = lambda b, i: (0, 0)
    return pl.pallas_call(
        functools.partial(_hyb_out_kernel, n_chunks=tc // RET_CHUNK, nsteps=nsteps),
        grid=(batch, nsteps),
        in_specs=[blk(0, D_MODEL),
                  blk(0), blk(1), blk(2), blk(3), blk(4), blk(5), blk(6), blk(0),
                  halo_prev(1), halo_prev(2), halo_next(1), halo_next(2),
                  pl.BlockSpec((1, RET_HEADS), const),
                  pl.BlockSpec((3, CONV_CH), const),
                  pl.BlockSpec((1, RET_WIDTH), const),
                  pl.BlockSpec((CONV_CH + RET_WIDTH, D_MODEL), const)],
        out_specs=blk(0, D_MODEL),
        out_shape=jax.ShapeDtypeStruct((t, D_MODEL), F32),
        scratch_shapes=[pltpu.VMEM((RET_HEADS, RET_DIM, RET_DIM), F32),
                        pltpu.VMEM((tc, CONV_CH + RET_WIDTH), BF16)],
        compiler_params=pltpu.CompilerParams(dimension_semantics=("arbitrary", "arbitrary"),
                                             vmem_limit_bytes=VMEM_LIMIT),
        name="hyb_out",
    )(x2, proj, proj, proj, proj, proj, proj, proj, o_f, proj, proj, proj, proj, dec, conv_w, gn_w, w_out)


def _ffn_kernel(*refs, pre_proj, final_norm):
    refs = list(refs)
    x_ref = refs.pop(0)
    if pre_proj:
        a_ref, wo_ref = refs.pop(0), refs.pop(0)
    nw_ref, wg_ref, wu_ref, wd_ref = refs[:4]
    refs = refs[4:]
    if final_norm:
        fw_ref = refs.pop(0)
    o_ref = refs.pop(0)

    x = x_ref[...]
    if pre_proj:
        x = x + jnp.dot(a_ref[...], wo_ref[...], preferred_element_type=F32)
    xn = _rms(x, nw_ref[...], NORM_EPS).astype(BF16)
    acc = x
    for c in range(D_FF // FF_CHUNK):
        cs = slice(c * FF_CHUNK, (c + 1) * FF_CHUNK)
        g = jnp.dot(xn, wg_ref[:, cs], preferred_element_type=F32)
        u = jnp.dot(xn, wu_ref[:, cs], preferred_element_type=F32)
        h = (jax.nn.silu(g) * u).astype(BF16)
        acc = acc + jnp.dot(h, wd_ref[cs, :], preferred_element_type=F32)
    if final_norm:
        acc = _rms(acc, fw_ref[...], NORM_EPS)
    o_ref[...] = acc


def _ffn(x2, nw, wg, wu, wd, tm, pre=None, final_w=None):
    t = x2.shape[0]
    row = lambda i: (i, 0)
    const = lambda i: (0, 0)
    single = pl.Buffered(1)
    args = [x2]
    specs = [pl.BlockSpec((tm, D_MODEL), row)]
    if pre is not None:
        a, wo = pre
        args += [a, wo]
        specs += [pl.BlockSpec((tm, D_MODEL), row),
                  pl.BlockSpec((D_MODEL, D_MODEL), const, pipeline_mode=single)]
    args += [nw, wg, wu, wd]
    specs += [pl.BlockSpec((1, D_MODEL), const),
              pl.BlockSpec((D_MODEL, D_FF), const, pipeline_mode=single),
              pl.BlockSpec((D_MODEL, D_FF), const, pipeline_mode=single),
              pl.BlockSpec((D_FF, D_MODEL), const, pipeline_mode=single)]
    if final_w is not None:
        args.append(final_w)
        specs.append(pl.BlockSpec((1, D_MODEL), const))
    return pl.pallas_call(
        functools.partial(_ffn_kernel, pre_proj=pre is not None, final_norm=final_w is not None),
        grid=(t // tm,),
        in_specs=specs,
        out_specs=pl.BlockSpec((tm, D_MODEL), row),
        out_shape=jax.ShapeDtypeStruct((t, D_MODEL), F32),
        compiler_params=pltpu.CompilerParams(dimension_semantics=("arbitrary",),
                                             vmem_limit_bytes=VMEM_LIMIT),
        name="ffn",
    )(*args)


FP8_TOP_EXP = 7
FP8_EXP_CLIP = 60


def _pow2_scale(m):
    exp = (lax.bitcast_convert_type(m, jnp.int32) >> 23) - 127
    es = jnp.clip(FP8_TOP_EXP - exp, -FP8_EXP_CLIP, FP8_EXP_CLIP)
    scale = lax.bitcast_convert_type((es + 127) << 23, F32)
    inv = lax.bitcast_convert_type((127 - es) << 23, F32)
    return scale, inv


def _fp8_split(x):
    hi = x.astype(F8).astype(F32)
    return hi, x - hi


def _absmax(x):
    return jnp.max(jnp.max(x, axis=0, keepdims=True), axis=1, keepdims=True)


def _qkv_kernel(x_ref, nw_ref, w_ref, cos_ref, slo_ref, shi_ref, qt_ref, kx_ref, kinv_ref, vt_ref):
    xn = _rms(x_ref[...], nw_ref[...], NORM_EPS).astype(BF16)
    cos, slo, shi = cos_ref[...], slo_ref[...], shi_ref[...]
    n_qk = D_MODEL // COL_CHUNK
    head_lane = lax.broadcasted_iota(jnp.int32, (1, 2 * DIFF_HEADS), 1)
    kinv = jnp.zeros((1, 2 * DIFF_HEADS), F32)
    for c in range(3 * n_qk):
        y = jnp.dot(xn, w_ref[:, c * COL_CHUNK:(c + 1) * COL_CHUNK], preferred_element_type=F32)
        kind, cc = divmod(c, n_qk)
        for g in range(COL_CHUNK // LANES):
            yg = y[:, g * LANES:(g + 1) * LANES]
            lo = cc * COL_CHUNK + g * LANES
            if kind == 0:
                r = _rope128(yg, cos, slo, shi, DIFF_DIM // 2) * (DIFF_DIM ** -0.5 * LOG2E)
                qt_ref[0, lo:lo + LANES, :] = r.T.astype(BF16)
            elif kind == 1:
                r = _rope128(yg, cos, slo, shi, DIFF_DIM // 2)
                first = lax.broadcasted_iota(jnp.int32, r.shape, 1) < DIFF_DIM
                a = jnp.abs(r)
                sc1, inv1 = _pow2_scale(_absmax(jnp.where(first, a, 0.0)))
                sc2, inv2 = _pow2_scale(_absmax(jnp.where(first, 0.0, a)))
                hi, lw = _fp8_split(r * jnp.where(first[0:1], sc1, sc2))
                g1 = jnp.where(first, hi, pltpu.roll(lw, DIFF_DIM, 1)).astype(F8)
                g2 = jnp.where(first, pltpu.roll(hi, DIFF_DIM, 1), lw).astype(F8)
                pair = lo // LANES
                base = pair * 4 * LANES
                kx_ref[:, base:base + LANES] = g1
                kx_ref[:, base + LANES:base + 2 * LANES] = g1
                kx_ref[:, base + 2 * LANES:base + 3 * LANES] = g2
                kx_ref[:, base + 3 * LANES:base + 4 * LANES] = g2
                kinv = jnp.where(head_lane == 2 * pair, inv1, kinv)
                kinv = jnp.where(head_lane == 2 * pair + 1, inv2, kinv)
            else:
                vt_ref[0, 0, lo:lo + LANES, :] = yg.T.astype(BF16)
    kinv_ref[0] = kinv


def _qkv(x2, nw, w_qkv, tabs, batch, seq, tm):
    t = x2.shape[0]
    spt = seq // tm
    row = lambda i: (i, 0)
    pos = lambda i: (i % spt, 0)
    const = lambda i: (0, 0)
    return pl.pallas_call(
        _qkv_kernel,
        grid=(t // tm,),
        in_specs=[pl.BlockSpec((tm, D_MODEL), row),
                  pl.BlockSpec((1, D_MODEL), const),
                  pl.BlockSpec((D_MODEL, 3 * D_MODEL), const),
                  pl.BlockSpec((tm, LANES), pos),
                  pl.BlockSpec((tm, LANES), pos),
                  pl.BlockSpec((tm, LANES), pos)],
        out_specs=[pl.BlockSpec((1, D_MODEL, tm), lambda i: (i // spt, 0, i % spt)),
                   pl.BlockSpec((tm, 4 * D_MODEL), row),
                   pl.BlockSpec((1, 1, 2 * DIFF_HEADS), lambda i: (i, 0, 0)),
                   pl.BlockSpec((1, 1, D_MODEL, tm), lambda i: (i // spt, i % spt, 0, 0))],
        out_shape=[jax.ShapeDtypeStruct((batch, D_MODEL, seq), BF16),
                   jax.ShapeDtypeStruct((t, 4 * D_MODEL), F8),
                   jax.ShapeDtypeStruct((t // tm, 1, 2 * DIFF_HEADS), F32),
                   jax.ShapeDtypeStruct((batch, spt, D_MODEL, tm), BF16)],
        compiler_params=pltpu.CompilerParams(dimension_semantics=("arbitrary",),
                                             vmem_limit_bytes=VMEM_LIMIT),
        name="qkv",
    )(x2, nw, w_qkv, *tabs)


ONES_ROWS = 16


def _attn_kernel(kinv_ref, qt_ref, kx_ref, vt_ref, lq1_ref, lk1_ref, lq2_ref, lk2_ref, sub_ref, o_ref,
                 qx_sc, s_a, s_b, mx_a, mx_b, m_sc, acc_sc, *, tk, nk, lambda_init):
    hd = 2 * DIFF_DIM
    qt = qt_ref[0].astype(F32)
    qinv = []
    for idx in range(2):
        qh = qt[idx * DIFF_DIM:(idx + 1) * DIFF_DIM]
        scale, inv = _pow2_scale(_absmax(jnp.abs(qh)))
        hi, lw = _fp8_split(qh * scale)
        qx_sc[idx] = jnp.concatenate([hi, hi, lw, lw], axis=0).astype(F8)
        qinv.append(inv)
    m_sc[...] = jnp.full_like(m_sc, -jnp.inf)
    acc_sc[...] = jnp.zeros_like(acc_sc)
    ones = jnp.ones((ONES_ROWS, tk), BF16)
    kinv_base = (pl.program_id(0) * nk) * (2 * DIFF_HEADS) + 2 * pl.program_id(1)

    subs = [(idx, slice(c0, c0 + MXU_COLS)) for idx in range(2) for c0 in range(0, qt.shape[1], MXU_COLS)]

    def scores_sub(j, s_ref, mx_ref, idx, cs):
        off = pl.multiple_of(j * tk, tk)
        kx = kx_ref[pl.ds(off, tk), idx * 4 * DIFF_DIM:(idx + 1) * 4 * DIFF_DIM]
        c = (kinv_ref[kinv_base + j * (2 * DIFF_HEADS) + idx] * qinv[idx]).astype(BF16)
        st = jnp.dot(kx, qx_sc[idx, :, cs], preferred_element_type=F32).astype(BF16) * c
        s_ref[idx, :, cs] = st
        mx_ref[idx, :, cs] = jnp.max(st, axis=0, keepdims=True).astype(F32)

    def accumulate_sub(j, s_ref, mx_ref, idx, cs):
        vt = jnp.concatenate([vt_ref[0, j], ones], axis=0)
        m_prev = m_sc[idx, :, cs]
        m_new = jnp.maximum(m_prev, mx_ref[idx, :, cs])
        alpha = jnp.exp2(m_prev - m_new)
        p = jnp.exp2(s_ref[idx, :, cs] - m_new.astype(BF16))
        acc_sc[idx, :, cs] = acc_sc[idx, :, cs] * alpha + jnp.dot(vt, p, preferred_element_type=F32)
        m_sc[idx, :, cs] = m_new

    bufs = ((s_a, mx_a), (s_b, mx_b))

    def step(j, parity):
        for idx, cs in subs:
            scores_sub(j + 1, *bufs[1 - parity], idx, cs)
            accumulate_sub(j, *bufs[parity], idx, cs)

    for idx, cs in subs:
        scores_sub(0, *bufs[0], idx, cs)

    def body(i, carry):
        for u in range(ATT_UNROLL):
            step(ATT_UNROLL * i + u, u % 2)
        return carry

    n_loop = (nk - 1) // ATT_UNROLL
    lax.fori_loop(0, n_loop, body, 0)
    for j in range(n_loop * ATT_UNROLL, nk - 1):
        step(j, j % 2)
    for idx, cs in subs:
        accumulate_sub(nk - 1, *bufs[(nk - 1) % 2], idx, cs)

    lam = (jnp.exp(jnp.sum(lq1_ref[...] * lk1_ref[...], axis=-1, keepdims=True))
           - jnp.exp(jnp.sum(lq2_ref[...] * lk2_ref[...], axis=-1, keepdims=True)) + lambda_init)
    o = (acc_sc[0, 0:hd, :] / acc_sc[0, hd:hd + 1, :]
         - lam * (acc_sc[1, 0:hd, :] / acc_sc[1, hd:hd + 1, :]))
    ms = jnp.mean(o * o, axis=0, keepdims=True)
    o = (o * lax.rsqrt(ms + SUBLN_EPS)).T
    o_ref[...] = (o * sub_ref[...] * (1.0 - lambda_init)).astype(BF16)


def _attn(qt, kx, kinv, vt, lq1, lk1, lq2, lk2, subln, batch, seq, tq, tk, lambda_init):
    nq = seq // tq
    nk = seq // tk
    hd = 2 * DIFF_DIM
    const = lambda b, h, i: (0, 0)
    return pl.pallas_call(
        functools.partial(_attn_kernel, tk=tk, nk=nk, lambda_init=lambda_init),
        grid=(batch, DIFF_HEADS, nq),
        in_specs=[pl.BlockSpec(memory_space=pltpu.SMEM),
                  pl.BlockSpec((1, hd, tq), lambda b, h, i: (b, h, i)),
                  pl.BlockSpec((seq, 4 * hd), lambda b, h, i: (b, h)),
                  pl.BlockSpec((1, nk, hd, tk), lambda b, h, i: (b, 0, h, 0)),
                  pl.BlockSpec((1, DIFF_DIM), const), pl.BlockSpec((1, DIFF_DIM), const),
                  pl.BlockSpec((1, DIFF_DIM), const), pl.BlockSpec((1, DIFF_DIM), const),
                  pl.BlockSpec((1, hd), const)],
        out_specs=pl.BlockSpec((tq, hd), lambda b, h, i: (b * nq + i, h)),
        out_shape=jax.ShapeDtypeStruct((batch * seq, DIFF_HEADS * hd), BF16),
        scratch_shapes=[pltpu.VMEM((2, 4 * DIFF_DIM, tq), F8),
                        pltpu.VMEM((2, tk, tq), BF16), pltpu.VMEM((2, tk, tq), BF16),
                        pltpu.VMEM((2, 1, tq), F32), pltpu.VMEM((2, 1, tq), F32),
                        pltpu.VMEM((2, 1, tq), F32),
                        pltpu.VMEM((2, hd + ONES_ROWS, tq), F32)],
        compiler_params=pltpu.CompilerParams(dimension_semantics=("arbitrary", "arbitrary", "arbitrary"),
                                             vmem_limit_bytes=VMEM_LIMIT),
        name="attn",
    )(kinv.reshape(-1), qt, kx, vt, lq1, lk1, lq2, lk2, subln)


def _trunk(x, p):
    batch, seq, _ = x.shape
    t = batch * seq
    tm = min(ROW_TILE, seq)
    tq = min(ATT_TQ, seq)
    x2 = x.reshape(t, D_MODEL)

    proj = _hyb_in(x2, p["norm_mix"][0:1], p["hyb_w_in"], _rope_tables(seq, RET_DIM), seq, tm)
    o_f = _ret_fwd(proj, p["hyb_decay_fwd"], batch, seq, tm)
    x2 = _hyb_out(x2, proj, o_f, p["hyb_decay_bwd"], p["hyb_conv_w"], p["hyb_gn"], p["hyb_w_out"],
                  batch, seq, tm)
    x2 = _ffn(x2, p["norm_ffn"][0:1], p["ffn_w_gate"][0], p["ffn_w_up"][0], p["ffn_w_down"][0], tm)

    lambda_init = 0.8 - 0.6 * math.exp(-0.3 * 1)
    qt, kx, kinv, vt = _qkv(x2, p["norm_mix"][1:2], p["diff_w_qkv"], _rope_tables(seq, DIFF_DIM), batch, seq, tm)
    att = _attn(qt, kx, kinv, vt, p["diff_lq1"], p["diff_lk1"], p["diff_lq2"], p["diff_lk2"], p["diff_subln"],
                batch, seq, tq, tm, lambda_init)
    x2 = _ffn(x2, p["norm_ffn"][1:2], p["ffn_w_gate"][1], p["ffn_w_up"][1], p["ffn_w_down"][1], tm,
              pre=(att, p["diff_w_out"]), final_w=p["norm_final"])
    return x2.reshape(batch, seq, D_MODEL)


def kernel(x_prompt, x_sample, norm_mix, norm_ffn, norm_final, hyb_w_in, hyb_conv_w, hyb_decay_fwd,
           hyb_decay_bwd, hyb_gn, hyb_w_out, diff_w_qkv, diff_lq1, diff_lk1, diff_lq2, diff_lk2,
           diff_subln, diff_w_out, ffn_w_gate, ffn_w_up, ffn_w_down):
    assert norm_mix.shape[0] == 2 and hyb_w_in.shape[0] == 1 and diff_w_qkv.shape[0] == 1
    p = {
        "norm_mix": norm_mix, "norm_ffn": norm_ffn, "norm_final": norm_final.reshape(1, D_MODEL),
        "hyb_w_in": hyb_w_in[0].astype(BF16), "hyb_conv_w": hyb_conv_w[0],
        "hyb_decay_fwd": hyb_decay_fwd, "hyb_decay_bwd": hyb_decay_bwd,
        "hyb_gn": hyb_gn, "hyb_w_out": hyb_w_out[0].astype(BF16),
        "diff_w_qkv": diff_w_qkv[0].astype(BF16),
        "diff_lq1": diff_lq1, "diff_lk1": diff_lk1, "diff_lq2": diff_lq2, "diff_lk2": diff_lk2,
        "diff_subln": diff_subln, "diff_w_out": diff_w_out[0].astype(BF16),
        "ffn_w_gate": ffn_w_gate.astype(BF16), "ffn_w_up": ffn_w_up.astype(BF16),
        "ffn_w_down": ffn_w_down.astype(BF16),
    }
    return (_trunk(x_prompt, p), _trunk(x_sample, p))
```

```python
import functools
import math

import jax
import jax.numpy as jnp
from jax import lax
from jax.experimental import pallas as pl
from jax.experimental.pallas import tpu as pltpu

F32 = jnp.float32
BF16 = jnp.bfloat16
F8 = jnp.float8_e4m3fn

D_MODEL = 1024
CONV_CH = 512
RET_HEADS = 4
RET_DIM = 128
RET_WIDTH = RET_HEADS * RET_DIM
RET_CHUNK = 128
DIFF_HEADS = 8
DIFF_DIM = 64
D_FF = 2816
ROPE_THETA = 10000.0
NORM_EPS = 1e-6
LOG2E = math.log2(math.e)
GN_EPS = 1e-5
SUBLN_EPS = 1e-5
IN_PROJ_COLS = 3 * CONV_CH + 4 * RET_WIDTH

LANES = 128
SUBLANES = 8
COL_CHUNK = 512
MXU_COLS = 256
FF_CHUNK = MXU_COLS
ROW_TILE = 512
ATT_TQ = 2048
ATT_TK = 512
ATT_MAX_CODE_STEPS = 11
ATT_GROUP = 1
VMEM_LIMIT = 56 * 1024 * 1024


def _rms(x, w, eps):
    ms = jnp.mean(x * x, axis=-1, keepdims=True)
    return (x * lax.rsqrt(ms + eps)) * w


def _rope_tables(seq, dim):
    half = dim // 2
    inv = ROPE_THETA ** (-jnp.arange(0, dim, 2, dtype=F32) / dim)
    ang = jnp.arange(seq, dtype=F32)[:, None] * inv[None, :]
    cos = jnp.cos(ang)
    sin = jnp.sin(ang)
    zero = jnp.zeros_like(sin)
    reps = LANES // dim
    cos_t = jnp.tile(jnp.concatenate([cos, cos], axis=1), (1, reps))
    sin_lo = jnp.tile(jnp.concatenate([-sin, zero], axis=1), (1, reps))
    sin_hi = jnp.tile(jnp.concatenate([zero, sin], axis=1), (1, reps))
    return cos_t, sin_lo, sin_hi


def _rope128(y, cos, sin_lo, sin_hi, half):
    return (y * cos + pltpu.roll(y, LANES - half, 1) * sin_lo + pltpu.roll(y, half, 1) * sin_hi)


def _hyb_in_kernel(x_ref, nw_ref, w_ref, cos_ref, slo_ref, shi_ref, o_ref):
    xn = _rms(x_ref[...], nw_ref[...], NORM_EPS).astype(BF16)
    cos, slo, shi = cos_ref[...], slo_ref[...], shi_ref[...]
    q_chunk = 3 * CONV_CH // COL_CHUNK
    for c in range(IN_PROJ_COLS // COL_CHUNK):
        y = jnp.dot(xn, w_ref[:, c * COL_CHUNK:(c + 1) * COL_CHUNK], preferred_element_type=F32)
        if c in (q_chunk, q_chunk + 1):
            scale = 1.0 if c == q_chunk else RET_DIM ** -0.5
            for g in range(COL_CHUNK // LANES):
                r = _rope128(y[:, g * LANES:(g + 1) * LANES], cos, slo, shi, RET_DIM // 2)
                if scale != 1.0:
                    r = r * scale
                o_ref[:, c * COL_CHUNK + g * LANES:c * COL_CHUNK + (g + 1) * LANES] = r.astype(BF16)
        else:
            o_ref[:, c * COL_CHUNK:(c + 1) * COL_CHUNK] = y.astype(BF16)


def _hyb_in(x2, nw, w_in, tabs, seq, tm):
    t = x2.shape[0]
    spt = seq // tm
    row = lambda i: (i, 0)
    pos = lambda i: (i % spt, 0)
    const = lambda i: (0, 0)
    return pl.pallas_call(
        _hyb_in_kernel,
        grid=(t // tm,),
        in_specs=[pl.BlockSpec((tm, D_MODEL), row),
                  pl.BlockSpec((1, D_MODEL), const),
                  pl.BlockSpec((D_MODEL, IN_PROJ_COLS), const),
                  pl.BlockSpec((tm, LANES), pos),
                  pl.BlockSpec((tm, LANES), pos),
                  pl.BlockSpec((tm, LANES), pos)],
        out_specs=pl.BlockSpec((tm, IN_PROJ_COLS), row),
        out_shape=jax.ShapeDtypeStruct((t, IN_PROJ_COLS), BF16),
        compiler_params=pltpu.CompilerParams(dimension_semantics=("arbitrary",),
                                             vmem_limit_bytes=VMEM_LIMIT),
        name="hyb_in",
    )(x2, nw, w_in, *tabs)


def _decay_tables(lg, strict):
    c = RET_CHUNK
    ii = lax.broadcasted_iota(jnp.int32, (c, c), 0).astype(F32)
    jj = lax.broadcasted_iota(jnp.int32, (c, c), 1).astype(F32)
    if strict:
        rel = jj - ii
        mask = rel > 0
        kdec = jnp.exp(lg * ii)
        qdec = jnp.exp(lg * (c - ii))
    else:
        rel = ii - jj
        mask = rel >= 0
        kdec = jnp.exp(lg * (c - 1.0 - ii))
        qdec = jnp.exp(lg * (ii + 1.0))
    dintra = jnp.where(mask, jnp.exp(lg * jnp.maximum(rel, 0.0)), 0.0)
    return dintra, kdec, qdec, jnp.exp(lg * c)


def _ret_chunk(q, k, v, st, dintra, kdec, qdec, cdec):
    sc = lax.dot_general(q, k, (((1,), (1,)), ((), ())), preferred_element_type=F32) * dintra
    o = jnp.dot(sc.astype(BF16), v, preferred_element_type=F32)
    o = o + jnp.dot(q, st.astype(BF16), preferred_element_type=F32) * qdec
    kd_t = (k.astype(F32) * kdec).T.astype(BF16)
    kv = jnp.dot(kd_t, v, preferred_element_type=F32)
    return o, st * cdec + kv


def _ret_fwd_kernel(q_ref, k_ref, v_ref, dec_ref, of_ref, state_sc, *, n_chunks):
    @pl.when(pl.program_id(1) == 0)
    def _():
        state_sc[...] = jnp.zeros_like(state_sc)

    for h in range(RET_HEADS):
        lg = -jnp.exp(dec_ref[:, h:h + 1])
        tabs = _decay_tables(lg, strict=False)
        cols = slice(h * RET_DIM, (h + 1) * RET_DIM)
        for c in range(n_chunks):
            rows = slice(c * RET_CHUNK, (c + 1) * RET_CHUNK)
            o, st = _ret_chunk(q_ref[rows, cols], k_ref[rows, cols], v_ref[rows, cols], state_sc[h], *tabs)
            of_ref[rows, cols] = o
            state_sc[h] = st


def _ret_fwd(proj, dec, batch, seq, tc):
    t = proj.shape[0]
    nsteps = seq // tc
    qb = 3 * CONV_CH // RET_WIDTH

    def blk(col):
        return pl.BlockSpec((tc, RET_WIDTH), lambda b, i: (b * nsteps + i, col))

    return pl.pallas_call(
        functools.partial(_ret_fwd_kernel, n_chunks=tc // RET_CHUNK),
        grid=(batch, nsteps),
        in_specs=[blk(qb), blk(qb + 1), blk(qb + 2), pl.BlockSpec((1, RET_HEADS), lambda b, i: (0, 0))],
        out_specs=pl.BlockSpec((tc, RET_WIDTH), lambda b, i: (b * nsteps + i, 0)),
        out_shape=jax.ShapeDtypeStruct((t, RET_WIDTH), F32),
        scratch_shapes=[pltpu.VMEM((RET_HEADS, RET_DIM, RET_DIM), F32)],
        compiler_params=pltpu.CompilerParams(dimension_semantics=("arbitrary", "arbitrary"),
                                             vmem_limit_bytes=VMEM_LIMIT),
        name="ret_fwd",
    )(proj, proj, proj, dec)


def _hyb_out_kernel(x_ref, ab_ref, ac_ref, ah_ref, q_ref, k_ref, v_ref, g_ref, of_ref,
                    acp_ref, ahp_ref, acn_ref, ahn_ref, dec_ref, cw_ref, gn_ref, wo_ref,
                    o_ref, state_sc, y_sc, *, n_chunks, nsteps):
    step = pl.program_id(1)

    @pl.when(step == 0)
    def _():
        state_sc[...] = jnp.zeros_like(state_sc)

    for h in range(RET_HEADS):
        lg = -jnp.exp(dec_ref[:, h:h + 1])
        tabs = _decay_tables(lg, strict=True)
        cols = slice(h * RET_DIM, (h + 1) * RET_DIM)
        gn_w = gn_ref[:, cols]
        for c in reversed(range(n_chunks)):
            rows = slice(c * RET_CHUNK, (c + 1) * RET_CHUNK)
            o, st = _ret_chunk(q_ref[rows, cols], k_ref[rows, cols], v_ref[rows, cols], state_sc[h], *tabs)
            state_sc[h] = st
            o = o + of_ref[rows, cols]
            mu = jnp.mean(o, axis=-1, keepdims=True)
            d = o - mu
            var = jnp.mean(d * d, axis=-1, keepdims=True)
            on = d * lax.rsqrt(var + GN_EPS) * gn_w
            g = g_ref[rows, cols].astype(F32)
            y_sc[rows, CONV_CH + h * RET_DIM:CONV_CH + (h + 1) * RET_DIM] = (jax.nn.silu(g) * on).astype(BF16)

    tc = n_chunks * RET_CHUNK
    tile = nsteps - 1 - step
    u = ac_ref[...].astype(F32) * ah_ref[...].astype(F32)
    u_prev = acp_ref[SUBLANES - 1:SUBLANES, :].astype(F32) * ahp_ref[SUBLANES - 1:SUBLANES, :].astype(F32)
    u_next = acn_ref[0:1, :].astype(F32) * ahn_ref[0:1, :].astype(F32)
    u_prev = jnp.where(tile > 0, u_prev, 0.0)
    u_next = jnp.where(tile < nsteps - 1, u_next, 0.0)
    ridx = lax.broadcasted_iota(jnp.int32, u.shape, 0)
    up = jnp.where(ridx == 0, u_prev, pltpu.roll(u, 1, 0))
    un = jnp.where(ridx == tc - 1, u_next, pltpu.roll(u, tc - 1, 0))
    conv = cw_ref[0:1, :] * up + cw_ref[1:2, :] * u + cw_ref[2:3, :] * un
    y_sc[:, 0:CONV_CH] = (ab_ref[...].astype(F32) * conv).astype(BF16)

    o_ref[...] = x_ref[...] + jnp.dot(y_sc[...], wo_ref[...], preferred_element_type=F32)


def _hyb_out(x2, proj, o_f, dec, conv_w, gn_w, w_out, batch, seq, tc):
    t = x2.shape[0]
    nsteps = seq // tc
    rpt = tc // SUBLANES
    n8 = t // SUBLANES
    tile = lambda b, i: b * nsteps + (nsteps - 1 - i)

    def blk(col, width=RET_WIDTH):
        return pl.BlockSpec((tc, width), lambda b, i: (tile(b, i), col))

    def halo_prev(col):
        return pl.BlockSpec((SUBLANES, CONV_CH), lambda b, i: (jnp.maximum(tile(b, i) * rpt - 1, 0), col))

    def halo_next(col):
        return pl.BlockSpec((SUBLANES, CONV_CH), lambda b, i: (jnp.minimum((tile(b, i) + 1) * rpt, n8 - 1), col))

    const = lambda b, i: (0, 0)
    return pl.pallas_call(
        functools.partial(_hyb_out_kernel, n_chunks=tc // RET_CHUNK, nsteps=nsteps),
        grid=(batch, nsteps),
        in_specs=[blk(0, D_MODEL),
                  blk(0), blk(1), blk(2), blk(3), blk(4), blk(5), blk(6), blk(0),
                  halo_prev(1), halo_prev(2), halo_next(1), halo_next(2),
                  pl.BlockSpec((1, RET_HEADS), const),
                  pl.BlockSpec((3, CONV_CH), const),
                  pl.BlockSpec((1, RET_WIDTH), const),
                  pl.BlockSpec((CONV_CH + RET_WIDTH, D_MODEL), const)],
        out_specs=blk(0, D_MODEL),
        out_shape=jax.ShapeDtypeStruct((t, D_MODEL), F32),
        scratch_shapes=[pltpu.VMEM((RET_HEADS, RET_DIM, RET_DIM), F32),
                        pltpu.VMEM((tc, CONV_CH + RET_WIDTH), BF16)],
        compiler_params=pltpu.CompilerParams(dimension_semantics=("arbitrary", "arbitrary"),
                                             vmem_limit_bytes=VMEM_LIMIT),
        name="hyb_out",
    )(x2, proj, proj, proj, proj, proj, proj, proj, o_f, proj, proj, proj, proj, dec, conv_w, gn_w, w_out)


def _ffn_kernel(*refs, pre_proj, final_norm):
    refs = list(refs)
    x_ref = refs.pop(0)
    if pre_proj:
        a_ref, wo_ref = refs.pop(0), refs.pop(0)
    nw_ref, wg_ref, wu_ref, wd_ref = refs[:4]
    refs = refs[4:]
    if final_norm:
        fw_ref = refs.pop(0)
    o_ref = refs.pop(0)

    x = x_ref[...]
    if pre_proj:
        x = x + jnp.dot(a_ref[...], wo_ref[...], preferred_element_type=F32)
    xn = _rms(x, nw_ref[...], NORM_EPS).astype(BF16)
    acc = x
    for c in range(D_FF // FF_CHUNK):
        cs = slice(c * FF_CHUNK, (c + 1) * FF_CHUNK)
        g = jnp.dot(xn, wg_ref[:, cs], preferred_element_type=F32)
        u = jnp.dot(xn, wu_ref[:, cs], preferred_element_type=F32)
        h = (jax.nn.silu(g) * u).astype(BF16)
        acc = acc + jnp.dot(h, wd_ref[cs, :], preferred_element_type=F32)
    if final_norm:
        acc = _rms(acc, fw_ref[...], NORM_EPS)
    o_ref[...] = acc


def _ffn(x2, nw, wg, wu, wd, tm, pre=None, final_w=None):
    t = x2.shape[0]
    row = lambda i: (i, 0)
    const = lambda i: (0, 0)
    single = pl.Buffered(1)
    args = [x2]
    specs = [pl.BlockSpec((tm, D_MODEL), row)]
    if pre is not None:
        a, wo = pre
        args += [a, wo]
        specs += [pl.BlockSpec((tm, D_MODEL), row),
                  pl.BlockSpec((D_MODEL, D_MODEL), const, pipeline_mode=single)]
    args += [nw, wg, wu, wd]
    specs += [pl.BlockSpec((1, D_MODEL), const),
              pl.BlockSpec((D_MODEL, D_FF), const, pipeline_mode=single),
              pl.BlockSpec((D_MODEL, D_FF), const, pipeline_mode=single),
              pl.BlockSpec((D_FF, D_MODEL), const, pipeline_mode=single)]
    if final_w is not None:
        args.append(final_w)
        specs.append(pl.BlockSpec((1, D_MODEL), const))
    return pl.pallas_call(
        functools.partial(_ffn_kernel, pre_proj=pre is not None, final_norm=final_w is not None),
        grid=(t // tm,),
        in_specs=specs,
        out_specs=pl.BlockSpec((tm, D_MODEL), row),
        out_shape=jax.ShapeDtypeStruct((t, D_MODEL), F32),
        compiler_params=pltpu.CompilerParams(dimension_semantics=("arbitrary",),
                                             vmem_limit_bytes=VMEM_LIMIT),
        name="ffn",
    )(*args)


FP8_TOP_EXP = 7
FP8_EXP_CLIP = 60


def _pow2_scale(m):
    exp = (lax.bitcast_convert_type(m, jnp.int32) >> 23) - 127
    es = jnp.clip(FP8_TOP_EXP - exp, -FP8_EXP_CLIP, FP8_EXP_CLIP)
    scale = lax.bitcast_convert_type((es + 127) << 23, F32)
    inv = lax.bitcast_convert_type((127 - es) << 23, F32)
    return scale, inv


def _fp8_split(x):
    hi = x.astype(F8).astype(F32)
    return hi, x - hi


def _absmax(x):
    return jnp.max(jnp.max(x, axis=0, keepdims=True), axis=1, keepdims=True)


def _qkv_kernel(x_ref, nw_ref, w_ref, cos_ref, slo_ref, shi_ref, qt_ref, kx_ref, kinv_ref, vt_ref):
    xn = _rms(x_ref[...], nw_ref[...], NORM_EPS).astype(BF16)
    cos, slo, shi = cos_ref[...], slo_ref[...], shi_ref[...]
    n_qk = D_MODEL // COL_CHUNK
    head_lane = lax.broadcasted_iota(jnp.int32, (1, 2 * DIFF_HEADS), 1)
    kinv = jnp.zeros((1, 2 * DIFF_HEADS), F32)
    for c in range(3 * n_qk):
        y = jnp.dot(xn, w_ref[:, c * COL_CHUNK:(c + 1) * COL_CHUNK], preferred_element_type=F32)
        kind, cc = divmod(c, n_qk)
        for g in range(COL_CHUNK // LANES):
            yg = y[:, g * LANES:(g + 1) * LANES]
            lo = cc * COL_CHUNK + g * LANES
            if kind == 0:
                r = _rope128(yg, cos, slo, shi, DIFF_DIM // 2) * (DIFF_DIM ** -0.5 * LOG2E)
                qt_ref[0, lo:lo + LANES, :] = r.T.astype(BF16)
            elif kind == 1:
                r = _rope128(yg, cos, slo, shi, DIFF_DIM // 2)
                first = lax.broadcasted_iota(jnp.int32, r.shape, 1) < DIFF_DIM
                a = jnp.abs(r)
                sc1, inv1 = _pow2_scale(_absmax(jnp.where(first, a, 0.0)))
                sc2, inv2 = _pow2_scale(_absmax(jnp.where(first, 0.0, a)))
                hi, lw = _fp8_split(r * jnp.where(first[0:1], sc1, sc2))
                g1 = jnp.where(first, hi, pltpu.roll(lw, DIFF_DIM, 1)).astype(F8)
                g2 = jnp.where(first, pltpu.roll(hi, DIFF_DIM, 1), lw).astype(F8)
                pair = lo // LANES
                base = pair * 4 * LANES
                kx_ref[:, base:base + LANES] = g1
                kx_ref[:, base + LANES:base + 2 * LANES] = g1
                kx_ref[:, base + 2 * LANES:base + 3 * LANES] = g2
                kx_ref[:, base + 3 * LANES:base + 4 * LANES] = g2
                kinv = jnp.where(head_lane == 2 * pair, inv1, kinv)
                kinv = jnp.where(head_lane == 2 * pair + 1, inv2, kinv)
            else:
                vt_ref[0, 0, lo:lo + LANES, :] = yg.T.astype(BF16)
    kinv_ref[0] = kinv


def _qkv(x2, nw, w_qkv, tabs, batch, seq, tm):
    t = x2.shape[0]
    spt = seq // tm
    row = lambda i: (i, 0)
    pos = lambda i: (i % spt, 0)
    const = lambda i: (0, 0)
    return pl.pallas_call(
        _qkv_kernel,
        grid=(t // tm,),
        in_specs=[pl.BlockSpec((tm, D_MODEL), row),
                  pl.BlockSpec((1, D_MODEL), const),
                  pl.BlockSpec((D_MODEL, 3 * D_MODEL), const),
                  pl.BlockSpec((tm, LANES), pos),
                  pl.BlockSpec((tm, LANES), pos),
                  pl.BlockSpec((tm, LANES), pos)],
        out_specs=[pl.BlockSpec((1, D_MODEL, tm), lambda i: (i // spt, 0, i % spt)),
                   pl.BlockSpec((tm, 4 * D_MODEL), row),
                   pl.BlockSpec((1, 1, 2 * DIFF_HEADS), lambda i: (i, 0, 0)),
                   pl.BlockSpec((1, 1, D_MODEL, tm), lambda i: (i // spt, i % spt, 0, 0))],
        out_shape=[jax.ShapeDtypeStruct((batch, D_MODEL, seq), BF16),
                   jax.ShapeDtypeStruct((t, 4 * D_MODEL), F8),
                   jax.ShapeDtypeStruct((t // tm, 1, 2 * DIFF_HEADS), F32),
                   jax.ShapeDtypeStruct((batch, spt, D_MODEL, tm), BF16)],
        compiler_params=pltpu.CompilerParams(dimension_semantics=("arbitrary",),
                                             vmem_limit_bytes=VMEM_LIMIT),
        name="qkv",
    )(x2, nw, w_qkv, *tabs)


ONES_ROWS = 16


def _attn_kernel(kinv_ref, qt_ref, kx_ref, vt_ref, lq1_ref, lk1_ref, lq2_ref, lk2_ref, sub_ref, o_ref,
                 qx_sc, s_a, s_b, mx_a, mx_b, m_sc, acc_sc, *, tk, nk, unroll, lambda_init):
    hd = 2 * DIFF_DIM
    qt = qt_ref[0].astype(F32)
    qinv = []
    for idx in range(2):
        qh = qt[idx * DIFF_DIM:(idx + 1) * DIFF_DIM]
        scale, inv = _pow2_scale(_absmax(jnp.abs(qh)))
        hi, lw = _fp8_split(qh * scale)
        qx_sc[idx] = jnp.concatenate([hi, hi, lw, lw], axis=0).astype(F8)
        qinv.append(inv)
    m_sc[...] = jnp.full_like(m_sc, -jnp.inf)
    acc_sc[...] = jnp.zeros_like(acc_sc)
    ones = jnp.ones((ONES_ROWS, tk), BF16)
    kinv_base = (pl.program_id(0) * nk) * (2 * DIFF_HEADS) + 2 * pl.program_id(1)

    subs = [(idx, slice(c0, c0 + MXU_COLS)) for idx in range(2) for c0 in range(0, qt.shape[1], MXU_COLS)]

    def scores_sub(j, s_ref, mx_ref, idx, cs):
        off = pl.multiple_of(j * tk, tk)
        kx = kx_ref[pl.ds(off, tk), idx * 4 * DIFF_DIM:(idx + 1) * 4 * DIFF_DIM]
        st = jnp.dot(kx, qx_sc[idx, :, cs], preferred_element_type=F32).astype(BF16)
        s_ref[idx, :, cs] = st
        mx_ref[idx, :, cs] = jnp.max(st, axis=0, keepdims=True).astype(F32)

    def accumulate_sub(j, s_ref, mx_ref, idx, cs):
        vt = jnp.concatenate([vt_ref[0, j], ones], axis=0)
        c = kinv_ref[kinv_base + j * (2 * DIFF_HEADS) + idx] * qinv[idx]
        m_prev = m_sc[idx, :, cs]
        m_new = jnp.maximum(m_prev, mx_ref[idx, :, cs] * c)
        alpha = jnp.exp2(m_prev - m_new)
        p = jnp.exp2(s_ref[idx, :, cs] * c.astype(BF16) - m_new.astype(BF16))
        acc_sc[idx, :, cs] = acc_sc[idx, :, cs] * alpha + jnp.dot(vt, p, preferred_element_type=F32)
        m_sc[idx, :, cs] = m_new

    bufs = ((s_a, mx_a), (s_b, mx_b))

    def step(j, parity):
        for g0 in range(0, len(subs), ATT_GROUP):
            for idx, cs in subs[g0:g0 + ATT_GROUP]:
                accumulate_sub(j, *bufs[parity], idx, cs)
            for idx, cs in subs[g0:g0 + ATT_GROUP]:
                scores_sub(j + 1, *bufs[1 - parity], idx, cs)

    for idx, cs in subs:
        scores_sub(0, *bufs[0], idx, cs)

    def body(i, carry):
        for u in range(unroll):
            step(unroll * i + u, u % 2)
        return carry

    n_loop = (nk - 1) // unroll
    lax.fori_loop(0, n_loop, body, 0)
    for j in range(n_loop * unroll, nk - 1):
        step(j, j % 2)
    for idx, cs in subs:
        accumulate_sub(nk - 1, *bufs[(nk - 1) % 2], idx, cs)

    lam = (jnp.exp(jnp.sum(lq1_ref[...] * lk1_ref[...], axis=-1, keepdims=True))
           - jnp.exp(jnp.sum(lq2_ref[...] * lk2_ref[...], axis=-1, keepdims=True)) + lambda_init)
    o = (acc_sc[0, 0:hd, :] / acc_sc[0, hd:hd + 1, :]
         - lam * (acc_sc[1, 0:hd, :] / acc_sc[1, hd:hd + 1, :]))
    ms = jnp.mean(o * o, axis=0, keepdims=True)
    o = (o * lax.rsqrt(ms + SUBLN_EPS)).T
    o_ref[...] = (o * sub_ref[...] * (1.0 - lambda_init)).astype(BF16)


def _pick_unroll(nk):
    fits = [u for u in range(2, ATT_MAX_CODE_STEPS + 1, 2) if u + (nk - 1) % u <= ATT_MAX_CODE_STEPS]
    return max(fits)


def _attn(qt, kx, kinv, vt, lq1, lk1, lq2, lk2, subln, batch, seq, tq, tk, lambda_init):
    nq = seq // tq
    nk = seq // tk
    hd = 2 * DIFF_DIM
    const = lambda b, h, i: (0, 0)
    return pl.pallas_call(
        functools.partial(_attn_kernel, tk=tk, nk=nk, unroll=_pick_unroll(nk), lambda_init=lambda_init),
        grid=(batch, DIFF_HEADS, nq),
        in_specs=[pl.BlockSpec(memory_space=pltpu.SMEM),
                  pl.BlockSpec((1, hd, tq), lambda b, h, i: (b, h, i)),
                  pl.BlockSpec((seq, 4 * hd), lambda b, h, i: (b, h)),
                  pl.BlockSpec((1, nk, hd, tk), lambda b, h, i: (b, 0, h, 0)),
                  pl.BlockSpec((1, DIFF_DIM), const), pl.BlockSpec((1, DIFF_DIM), const),
                  pl.BlockSpec((1, DIFF_DIM), const), pl.BlockSpec((1, DIFF_DIM), const),
                  pl.BlockSpec((1, hd), const)],
        out_specs=pl.BlockSpec((tq, hd), lambda b, h, i: (b * nq + i, h)),
        out_shape=jax.ShapeDtypeStruct((batch * seq, DIFF_HEADS * hd), BF16),
        scratch_shapes=[pltpu.VMEM((2, 4 * DIFF_DIM, tq), F8),
                        pltpu.VMEM((2, tk, tq), BF16), pltpu.VMEM((2, tk, tq), BF16),
                        pltpu.VMEM((2, 1, tq), F32), pltpu.VMEM((2, 1, tq), F32),
                        pltpu.VMEM((2, 1, tq), F32),
                        pltpu.VMEM((2, hd + ONES_ROWS, tq), F32)],
        compiler_params=pltpu.CompilerParams(dimension_semantics=("arbitrary", "arbitrary", "arbitrary"),
                                             vmem_limit_bytes=VMEM_LIMIT),
        name="attn",
    )(kinv.reshape(-1), qt, kx, vt, lq1, lk1, lq2, lk2, subln)


def _trunk(x, p):
    batch, seq, _ = x.shape
    t = batch * seq
    tm = min(ROW_TILE, seq)
    tq = min(ATT_TQ, seq)
    x2 = x.reshape(t, D_MODEL)

    proj = _hyb_in(x2, p["norm_mix"][0:1], p["hyb_w_in"], _rope_tables(seq, RET_DIM), seq, tm)
    o_f = _ret_fwd(proj, p["hyb_decay_fwd"], batch, seq, tm)
    x2 = _hyb_out(x2, proj, o_f, p["hyb_decay_bwd"], p["hyb_conv_w"], p["hyb_gn"], p["hyb_w_out"],
                  batch, seq, tm)
    x2 = _ffn(x2, p["norm_ffn"][0:1], p["ffn_w_gate"][0], p["ffn_w_up"][0], p["ffn_w_down"][0], tm)

    lambda_init = 0.8 - 0.6 * math.exp(-0.3 * 1)
    qt, kx, kinv, vt = _qkv(x2, p["norm_mix"][1:2], p["diff_w_qkv"], _rope_tables(seq, DIFF_DIM), batch, seq, tm)
    att = _attn(qt, kx, kinv, vt, p["diff_lq1"], p["diff_lk1"], p["diff_lq2"], p["diff_lk2"], p["diff_subln"],
                batch, seq, tq, tm, lambda_init)
    x2 = _ffn(x2, p["norm_ffn"][1:2], p["ffn_w_gate"][1], p["ffn_w_up"][1], p["ffn_w_down"][1], tm,
              pre=(att, p["diff_w_out"]), final_w=p["norm_final"])
    return x2.reshape(batch, seq, D_MODEL)


def kernel(x_prompt, x_sample, norm_mix, norm_ffn, norm_final, hyb_w_in, hyb_conv_w, hyb_decay_fwd,
           hyb_decay_bwd, hyb_gn, hyb_w_out, diff_w_qkv, diff_lq1, diff_lk1, diff_lq2, diff_lk2,
           diff_subln, diff_w_out, ffn_w_gate, ffn_w_up, ffn_w_down):
    assert norm_mix.shape[0] == 2 and hyb_w_in.shape[0] == 1 and diff_w_qkv.shape[0] == 1
    p = {
        "norm_mix": norm_mix, "norm_ffn": norm_ffn, "norm_final": norm_final.reshape(1, D_MODEL),
        "hyb_w_in": hyb_w_in[0].astype(BF16), "hyb_conv_w": hyb_conv_w[0],
        "hyb_decay_fwd": hyb_decay_fwd, "hyb_decay_bwd": hyb_decay_bwd,
        "hyb_gn": hyb_gn, "hyb_w_out": hyb_w_out[0].astype(BF16),
        "diff_w_qkv": diff_w_qkv[0].astype(BF16),
        "diff_lq1": diff_lq1, "diff_lk1": diff_lk1, "diff_lq2": diff_lq2, "diff_lk2": diff_lk2,
        "diff_subln": diff_subln, "diff_w_out": diff_w_out[0].astype(BF16),
        "ffn_w_gate": ffn_w_gate.astype(BF16), "ffn_w_up": ffn_w_up.astype(BF16),
        "ffn_w_down": ffn_w_down.astype(BF16),
    }
    return (_trunk(x_prompt, p), _trunk(x_sample, p))
```

```python
import functools
import math

import jax
import jax.numpy as jnp
from jax import lax
from jax.experimental import pallas as pl
from jax.experimental.pallas import tpu as pltpu

F32 = jnp.float32
BF16 = jnp.bfloat16
F8 = jnp.float8_e4m3fn

D_MODEL = 1024
CONV_CH = 512
RET_HEADS = 4
RET_DIM = 128
RET_WIDTH = RET_HEADS * RET_DIM
RET_CHUNK = 128
DIFF_HEADS = 8
DIFF_DIM = 64
V_DIM = 2 * DIFF_DIM
D_FF = 2816
ROPE_THETA = 10000.0
NORM_EPS = 1e-6
LOG2E = math.log2(math.e)
GN_EPS = 1e-5
SUBLN_EPS = 1e-5
IN_PROJ_COLS = 3 * CONV_CH + 4 * RET_WIDTH

LANES = 128
SUBLANES = 8
COL_CHUNK = 512
MXU_COLS = 256
FF_CHUNK = MXU_COLS
ROW_TILE = 512
ATT_TQ = 2048
ATT_TK = 512
ATT_MAX_CODE_STEPS = 11
ATT_GROUP = 1
VMEM_LIMIT = 56 * 1024 * 1024


def _rms(x, w, eps):
    ms = jnp.mean(x * x, axis=-1, keepdims=True)
    return (x * lax.rsqrt(ms + eps)) * w


def _rope_tables(seq, dim):
    half = dim // 2
    inv = ROPE_THETA ** (-jnp.arange(0, dim, 2, dtype=F32) / dim)
    ang = jnp.arange(seq, dtype=F32)[:, None] * inv[None, :]
    cos = jnp.cos(ang)
    sin = jnp.sin(ang)
    zero = jnp.zeros_like(sin)
    reps = LANES // dim
    cos_t = jnp.tile(jnp.concatenate([cos, cos], axis=1), (1, reps))
    sin_lo = jnp.tile(jnp.concatenate([-sin, zero], axis=1), (1, reps))
    sin_hi = jnp.tile(jnp.concatenate([zero, sin], axis=1), (1, reps))
    return cos_t, sin_lo, sin_hi


def _rope_tables_t(seq, dim):
    inv = ROPE_THETA ** (-jnp.arange(0, dim, 2, dtype=F32) / dim)
    ang = jnp.arange(seq, dtype=F32)[:, None] * inv[None, :]
    return jnp.cos(ang).T, jnp.sin(ang).T


def _rope128(y, cos, sin_lo, sin_hi, half):
    return (y * cos + pltpu.roll(y, LANES - half, 1) * sin_lo + pltpu.roll(y, half, 1) * sin_hi)


def _hyb_in_kernel(x_ref, nw_ref, w_ref, cos_ref, slo_ref, shi_ref, dec_ref, o_ref, of_ref, state_sc, *, spt):
    @pl.when(pl.program_id(0) % spt == 0)
    def _():
        state_sc[...] = jnp.zeros_like(state_sc)

    xn = _rms(x_ref[...], nw_ref[...], NORM_EPS).astype(BF16)
    cos, slo, shi = cos_ref[...], slo_ref[...], shi_ref[...]
    q_chunk = 3 * CONV_CH // COL_CHUNK

    def project(c):
        y = jnp.dot(xn, w_ref[:, c * COL_CHUNK:(c + 1) * COL_CHUNK], preferred_element_type=F32)
        if c in (q_chunk, q_chunk + 1):
            scale = 1.0 if c == q_chunk else RET_DIM ** -0.5
            for g in range(COL_CHUNK // LANES):
                r = _rope128(y[:, g * LANES:(g + 1) * LANES], cos, slo, shi, RET_DIM // 2)
                if scale != 1.0:
                    r = r * scale
                o_ref[:, c * COL_CHUNK + g * LANES:c * COL_CHUNK + (g + 1) * LANES] = r.astype(BF16)
        else:
            o_ref[:, c * COL_CHUNK:(c + 1) * COL_CHUNK] = y.astype(BF16)

    for c in (q_chunk, q_chunk + 1, q_chunk + 2):
        project(c)

    tabs = [_decay_tables(-jnp.exp(dec_ref[:, h:h + 1]), strict=False) for h in range(RET_HEADS)]
    qb, kb, vb = (c * COL_CHUNK for c in (q_chunk, q_chunk + 1, q_chunk + 2))
    for c in range(x_ref.shape[0] // RET_CHUNK):
        rows = slice(c * RET_CHUNK, (c + 1) * RET_CHUNK)
        for h in range(RET_HEADS):
            lo = h * RET_DIM
            o, st = _ret_chunk(o_ref[rows, qb + lo:qb + lo + RET_DIM], o_ref[rows, kb + lo:kb + lo + RET_DIM],
                               o_ref[rows, vb + lo:vb + lo + RET_DIM], state_sc[h], *tabs[h])
            of_ref[rows, lo:lo + RET_DIM] = o
            state_sc[h] = st

    for c in range(IN_PROJ_COLS // COL_CHUNK):
        if c not in (q_chunk, q_chunk + 1, q_chunk + 2):
            project(c)


def _hyb_in(x2, nw, w_in, tabs, dec, seq, tm):
    t = x2.shape[0]
    spt = seq // tm
    row = lambda i: (i, 0)
    pos = lambda i: (i % spt, 0)
    const = lambda i: (0, 0)
    return pl.pallas_call(
        functools.partial(_hyb_in_kernel, spt=spt),
        grid=(t // tm,),
        in_specs=[pl.BlockSpec((tm, D_MODEL), row),
                  pl.BlockSpec((1, D_MODEL), const),
                  pl.BlockSpec((D_MODEL, IN_PROJ_COLS), const),
                  pl.BlockSpec((tm, LANES), pos),
                  pl.BlockSpec((tm, LANES), pos),
                  pl.BlockSpec((tm, LANES), pos),
                  pl.BlockSpec((1, RET_HEADS), const)],
        out_specs=[pl.BlockSpec((tm, IN_PROJ_COLS), row),
                   pl.BlockSpec((tm, RET_WIDTH), row)],
        out_shape=[jax.ShapeDtypeStruct((t, IN_PROJ_COLS), BF16),
                   jax.ShapeDtypeStruct((t, RET_WIDTH), F32)],
        scratch_shapes=[pltpu.VMEM((RET_HEADS, RET_DIM, RET_DIM), F32)],
        compiler_params=pltpu.CompilerParams(dimension_semantics=("arbitrary",),
                                             vmem_limit_bytes=VMEM_LIMIT),
        name="hyb_in",
    )(x2, nw, w_in, *tabs, dec)


def _decay_tables(lg, strict):
    c = RET_CHUNK
    ii = lax.broadcasted_iota(jnp.int32, (c, c), 0).astype(F32)
    jj = lax.broadcasted_iota(jnp.int32, (c, c), 1).astype(F32)
    if strict:
        rel = jj - ii
        mask = rel > 0
        kdec = jnp.exp(lg * ii)
        qdec = jnp.exp(lg * (c - ii))
    else:
        rel = ii - jj
        mask = rel >= 0
        kdec = jnp.exp(lg * (c - 1.0 - ii))
        qdec = jnp.exp(lg * (ii + 1.0))
    dintra = jnp.where(mask, jnp.exp(lg * jnp.maximum(rel, 0.0)), 0.0)
    return dintra, kdec, qdec, jnp.exp(lg * c)


def _ret_chunk(q, k, v, st, dintra, kdec, qdec, cdec):
    sc = lax.dot_general(q, k, (((1,), (1,)), ((), ())), preferred_element_type=F32) * dintra
    o = jnp.dot(sc.astype(BF16), v, preferred_element_type=F32)
    o = o + jnp.dot(q, st.astype(BF16), preferred_element_type=F32) * qdec
    kd_t = (k.astype(F32) * kdec).T.astype(BF16)
    kv = jnp.dot(kd_t, v, preferred_element_type=F32)
    return o, st * cdec + kv


def _hyb_out_kernel(x_ref, ab_ref, ac_ref, ah_ref, q_ref, k_ref, v_ref, g_ref, of_ref,
                    acp_ref, ahp_ref, acn_ref, ahn_ref, dec_ref, cw_ref, gn_ref, wo_ref,
                    o_ref, state_sc, y_sc, *, n_chunks, nsteps):
    step = pl.program_id(1)

    @pl.when(step == 0)
    def _():
        state_sc[...] = jnp.zeros_like(state_sc)

    tabs = [_decay_tables(-jnp.exp(dec_ref[:, h:h + 1]), strict=True) for h in range(RET_HEADS)]
    for c in reversed(range(n_chunks)):
        rows = slice(c * RET_CHUNK, (c + 1) * RET_CHUNK)
        for h in range(RET_HEADS):
            cols = slice(h * RET_DIM, (h + 1) * RET_DIM)
            gn_w = gn_ref[:, cols]
            o, st = _ret_chunk(q_ref[rows, cols], k_ref[rows, cols], v_ref[rows, cols], state_sc[h], *tabs[h])
            state_sc[h] = st
            o = o + of_ref[rows, cols]
            mu = jnp.mean(o, axis=-1, keepdims=True)
            d = o - mu
            var = jnp.mean(d * d, axis=-1, keepdims=True)
            on = d * lax.rsqrt(var + GN_EPS) * gn_w
            g = g_ref[rows, cols].astype(F32)
            y_sc[rows, CONV_CH + h * RET_DIM:CONV_CH + (h + 1) * RET_DIM] = (jax.nn.silu(g) * on).astype(BF16)

    tc = n_chunks * RET_CHUNK
    tile = nsteps - 1 - step
    u = ac_ref[...].astype(F32) * ah_ref[...].astype(F32)
    u_prev = acp_ref[SUBLANES - 1:SUBLANES, :].astype(F32) * ahp_ref[SUBLANES - 1:SUBLANES, :].astype(F32)
    u_next = acn_ref[0:1, :].astype(F32) * ahn_ref[0:1, :].astype(F32)
    u_prev = jnp.where(tile > 0, u_prev, 0.0)
    u_next = jnp.where(tile < nsteps - 1, u_next, 0.0)
    ridx = lax.broadcasted_iota(jnp.int32, u.shape, 0)
    up = jnp.where(ridx == 0, u_prev, pltpu.roll(u, 1, 0))
    un = jnp.where(ridx == tc - 1, u_next, pltpu.roll(u, tc - 1, 0))
    conv = cw_ref[0:1, :] * up + cw_ref[1:2, :] * u + cw_ref[2:3, :] * un
    y_sc[:, 0:CONV_CH] = (ab_ref[...].astype(F32) * conv).astype(BF16)

    o_ref[...] = x_ref[...] + jnp.dot(y_sc[...], wo_ref[...], preferred_element_type=F32)


def _hyb_out(x2, proj, o_f, dec, conv_w, gn_w, w_out, batch, seq, tc):
    t = x2.shape[0]
    nsteps = seq // tc
    rpt = tc // SUBLANES
    n8 = t // SUBLANES
    tile = lambda b, i: b * nsteps + (nsteps - 1 - i)

    def blk(col, width=RET_WIDTH):
        return pl.BlockSpec((tc, width), lambda b, i: (tile(b, i), col))

    def halo_prev(col):
        return pl.BlockSpec((SUBLANES, CONV_CH), lambda b, i: (jnp.maximum(tile(b, i) * rpt - 1, 0), col))

    def halo_next(col):
        return pl.BlockSpec((SUBLANES, CONV_CH), lambda b, i: (jnp.minimum((tile(b, i) + 1) * rpt, n8 - 1), col))

    const = lambda b, i: (0, 0)
    return pl.pallas_call(
        functools.partial(_hyb_out_kernel, n_chunks=tc // RET_CHUNK, nsteps=nsteps),
        grid=(batch, nsteps),
        in_specs=[blk(0, D_MODEL),
                  blk(0), blk(1), blk(2), blk(3), blk(4), blk(5), blk(6), blk(0),
                  halo_prev(1), halo_prev(2), halo_next(1), halo_next(2),
                  pl.BlockSpec((1, RET_HEADS), const),
                  pl.BlockSpec((3, CONV_CH), const),
                  pl.BlockSpec((1, RET_WIDTH), const),
                  pl.BlockSpec((CONV_CH + RET_WIDTH, D_MODEL), const)],
        out_specs=blk(0, D_MODEL),
        out_shape=jax.ShapeDtypeStruct((t, D_MODEL), F32),
        scratch_shapes=[pltpu.VMEM((RET_HEADS, RET_DIM, RET_DIM), F32),
                        pltpu.VMEM((tc, CONV_CH + RET_WIDTH), BF16)],
        compiler_params=pltpu.CompilerParams(dimension_semantics=("arbitrary", "arbitrary"),
                                             vmem_limit_bytes=VMEM_LIMIT),
        name="hyb_out",
    )(x2, proj, proj, proj, proj, proj, proj, proj, o_f, proj, proj, proj, proj, dec, conv_w, gn_w, w_out)


def _ffn_kernel(*refs, pre_proj, final_norm):
    refs = list(refs)
    x_ref = refs.pop(0)
    if pre_proj:
        a_ref, wo_ref = refs.pop(0), refs.pop(0)
    nw_ref, wg_ref, wu_ref, wd_ref = refs[:4]
    refs = refs[4:]
    if final_norm:
        fw_ref = refs.pop(0)
    o_ref = refs.pop(0)

    x = x_ref[...]
    if pre_proj:
        x = x + jnp.dot(a_ref[...], wo_ref[...], preferred_element_type=F32)
    xn = _rms(x, nw_ref[...], NORM_EPS).astype(BF16)
    acc = x
    for c in range(D_FF // FF_CHUNK):
        cs = slice(c * FF_CHUNK, (c + 1) * FF_CHUNK)
        g = jnp.dot(xn, wg_ref[:, cs], preferred_element_type=F32)
        u = jnp.dot(xn, wu_ref[:, cs], preferred_element_type=F32)
        h = (jax.nn.silu(g) * u).astype(BF16)
        acc = acc + jnp.dot(h, wd_ref[cs, :], preferred_element_type=F32)
    if final_norm:
        acc = _rms(acc, fw_ref[...], NORM_EPS)
    o_ref[...] = acc


def _ffn(x2, nw, wg, wu, wd, tm, pre=None, final_w=None):
    t = x2.shape[0]
    row = lambda i: (i, 0)
    const = lambda i: (0, 0)
    single = pl.Buffered(1)
    args = [x2]
    specs = [pl.BlockSpec((tm, D_MODEL), row)]
    if pre is not None:
        a, wo = pre
        args += [a, wo]
        specs += [pl.BlockSpec((tm, D_MODEL), row),
                  pl.BlockSpec((D_MODEL, D_MODEL), const, pipeline_mode=single)]
    args += [nw, wg, wu, wd]
    specs += [pl.BlockSpec((1, D_MODEL), const),
              pl.BlockSpec((D_MODEL, D_FF), const, pipeline_mode=single),
              pl.BlockSpec((D_MODEL, D_FF), const, pipeline_mode=single),
              pl.BlockSpec((D_FF, D_MODEL), const, pipeline_mode=single)]
    if final_w is not None:
        args.append(final_w)
        specs.append(pl.BlockSpec((1, D_MODEL), const))
    return pl.pallas_call(
        functools.partial(_ffn_kernel, pre_proj=pre is not None, final_norm=final_w is not None),
        grid=(t // tm,),
        in_specs=specs,
        out_specs=pl.BlockSpec((tm, D_MODEL), row),
        out_shape=jax.ShapeDtypeStruct((t, D_MODEL), F32),
        compiler_params=pltpu.CompilerParams(dimension_semantics=("arbitrary",),
                                             vmem_limit_bytes=VMEM_LIMIT),
        name="ffn",
    )(*args)


FP8_TOP_EXP = 7
FP8_EXP_CLIP = 60


def _pow2_scale(m):
    exp = (lax.bitcast_convert_type(m, jnp.int32) >> 23) - 127
    es = jnp.clip(FP8_TOP_EXP - exp, -FP8_EXP_CLIP, FP8_EXP_CLIP)
    scale = lax.bitcast_convert_type((es + 127) << 23, F32)
    inv = lax.bitcast_convert_type((127 - es) << 23, F32)
    return scale, inv


def _fp8_split(x):
    hi = x.astype(F8).astype(F32)
    return hi, x - hi


def _absmax(x):
    return jnp.max(jnp.max(x, axis=0, keepdims=True), axis=1, keepdims=True)


def _qkv_kernel(x_ref, nw_ref, wqt_ref, wk_ref, wvt_ref, cos_ref, slo_ref, shi_ref, cost_ref, sint_ref,
                qt_ref, kx_ref, kinv_ref, vt_ref):
    xn = _rms(x_ref[...], nw_ref[...], NORM_EPS).astype(BF16)
    xnt = xn.T
    half = DIFF_DIM // 2

    cost, sint = cost_ref[...], sint_ref[...]
    for c in range(D_MODEL // COL_CHUNK):
        y = jnp.dot(wqt_ref[c * COL_CHUNK:(c + 1) * COL_CHUNK, :], xnt, preferred_element_type=F32)
        for hrow in range(0, COL_CHUNK, DIFF_DIM):
            x1, x2 = y[hrow:hrow + half], y[hrow + half:hrow + DIFF_DIM]
            lo = c * COL_CHUNK + hrow
            qscale = DIFF_DIM ** -0.5 * LOG2E
            qt_ref[0, lo:lo + half, :] = ((x1 * cost - x2 * sint) * qscale).astype(BF16)
            qt_ref[0, lo + half:lo + DIFF_DIM, :] = ((x2 * cost + x1 * sint) * qscale).astype(BF16)

    cos, slo, shi = cos_ref[...], slo_ref[...], shi_ref[...]
    head_lane = lax.broadcasted_iota(jnp.int32, (1, 2 * DIFF_HEADS), 1)
    kinv = jnp.zeros((1, 2 * DIFF_HEADS), F32)
    for c in range(D_MODEL // COL_CHUNK):
        y = jnp.dot(xn, wk_ref[:, c * COL_CHUNK:(c + 1) * COL_CHUNK], preferred_element_type=F32)
        for g in range(COL_CHUNK // LANES):
            r = _rope128(y[:, g * LANES:(g + 1) * LANES], cos, slo, shi, half)
            first = lax.broadcasted_iota(jnp.int32, r.shape, 1) < DIFF_DIM
            colmax = jnp.max(jnp.abs(r), axis=0, keepdims=True)
            sc1, inv1 = _pow2_scale(jnp.max(jnp.where(first[0:1], colmax, 0.0), axis=1, keepdims=True))
            sc2, inv2 = _pow2_scale(jnp.max(jnp.where(first[0:1], 0.0, colmax), axis=1, keepdims=True))
            hi, lw = _fp8_split(r * jnp.where(first[0:1], sc1, sc2))
            g1 = jnp.where(first, hi, pltpu.roll(lw, DIFF_DIM, 1)).astype(F8)
            g2 = jnp.where(first, pltpu.roll(hi, DIFF_DIM, 1), lw).astype(F8)
            pair = c * (COL_CHUNK // LANES) + g
            base = pair * 4 * LANES
            kx_ref[:, base:base + LANES] = g1
            kx_ref[:, base + LANES:base + 2 * LANES] = g1
            kx_ref[:, base + 2 * LANES:base + 3 * LANES] = g2
            kx_ref[:, base + 3 * LANES:base + 4 * LANES] = g2
            kinv = jnp.where(head_lane == 2 * pair, inv1, kinv)
            kinv = jnp.where(head_lane == 2 * pair + 1, inv2, kinv)
    kinv_ref[0] = kinv

    for c in range(D_MODEL // COL_CHUNK):
        y = jnp.dot(wvt_ref[c * COL_CHUNK:(c + 1) * COL_CHUNK, :], xnt, preferred_element_type=F32)
        for r0 in range(0, COL_CHUNK, V_DIM):
            head = (c * COL_CHUNK + r0) // V_DIM
            vt_ref[0, 0, head, 0:V_DIM, :] = y[r0:r0 + V_DIM].astype(BF16)
            vt_ref[0, 0, head, V_DIM:V_DIM + ONES_ROWS, :] = jnp.ones((ONES_ROWS, x_ref.shape[0]), BF16)


def _qkv(x2, nw, wq_t, wk, wv_t, tabs, tabs_t, batch, seq, tm):
    t = x2.shape[0]
    spt = seq // tm
    row = lambda i: (i, 0)
    pos = lambda i: (i % spt, 0)
    pos_t = lambda i: (0, i % spt)
    const = lambda i: (0, 0)
    sq = (D_MODEL, D_MODEL)
    return pl.pallas_call(
        _qkv_kernel,
        grid=(t // tm,),
        in_specs=[pl.BlockSpec((tm, D_MODEL), row),
                  pl.BlockSpec((1, D_MODEL), const),
                  pl.BlockSpec(sq, const), pl.BlockSpec(sq, const), pl.BlockSpec(sq, const),
                  pl.BlockSpec((tm, LANES), pos),
                  pl.BlockSpec((tm, LANES), pos),
                  pl.BlockSpec((tm, LANES), pos),
                  pl.BlockSpec((DIFF_DIM // 2, tm), pos_t),
                  pl.BlockSpec((DIFF_DIM // 2, tm), pos_t)],
        out_specs=[pl.BlockSpec((1, D_MODEL, tm), lambda i: (i // spt, 0, i % spt)),
                   pl.BlockSpec((tm, 4 * D_MODEL), row),
                   pl.BlockSpec((1, 1, 2 * DIFF_HEADS), lambda i: (i, 0, 0)),
                   pl.BlockSpec((1, 1, DIFF_HEADS, V_DIM + ONES_ROWS, tm), lambda i: (i // spt, i % spt, 0, 0, 0))],
        out_shape=[jax.ShapeDtypeStruct((batch, D_MODEL, seq), BF16),
                   jax.ShapeDtypeStruct((t, 4 * D_MODEL), F8),
                   jax.ShapeDtypeStruct((t // tm, 1, 2 * DIFF_HEADS), F32),
                   jax.ShapeDtypeStruct((batch, spt, DIFF_HEADS, V_DIM + ONES_ROWS, tm), BF16)],
        compiler_params=pltpu.CompilerParams(dimension_semantics=("arbitrary",),
                                             vmem_limit_bytes=VMEM_LIMIT),
        name="qkv",
    )(x2, nw, wq_t, wk, wv_t, *tabs, *tabs_t)


ONES_ROWS = 16


def _attn_kernel(kinv_ref, qt_ref, kx_ref, vt_ref, lq1_ref, lk1_ref, lq2_ref, lk2_ref, sub_ref, o_ref,
                 qx_sc, s_a, s_b, mx_a, mx_b, m_sc, acc_sc, *, tk, nk, unroll, lambda_init):
    hd = 2 * DIFF_DIM
    qt = qt_ref[0].astype(F32)
    qinv = []
    for idx in range(2):
        qh = qt[idx * DIFF_DIM:(idx + 1) * DIFF_DIM]
        scale, inv = _pow2_scale(_absmax(jnp.abs(qh)))
        hi, lw = _fp8_split(qh * scale)
        qx_sc[idx] = jnp.concatenate([hi, hi, lw, lw], axis=0).astype(F8)
        qinv.append(inv)
    m_sc[...] = jnp.full_like(m_sc, -jnp.inf)
    acc_sc[...] = jnp.zeros_like(acc_sc)
    kinv_base = (pl.program_id(0) * nk) * (2 * DIFF_HEADS) + 2 * pl.program_id(1)

    subs = [(idx, slice(c0, c0 + MXU_COLS)) for idx in range(2) for c0 in range(0, qt.shape[1], MXU_COLS)]

    def scores_sub(j, s_ref, mx_ref, idx, cs):
        off = pl.multiple_of(j * tk, tk)
        kx = kx_ref[pl.ds(off, tk), idx * 4 * DIFF_DIM:(idx + 1) * 4 * DIFF_DIM]
        st = jnp.dot(kx, qx_sc[idx, :, cs], preferred_element_type=F32).astype(BF16)
        s_ref[idx, :, cs] = st
        mx_ref[idx, :, cs] = jnp.max(st, axis=0, keepdims=True).astype(F32)

    def accumulate_sub(j, s_ref, mx_ref, idx, cs):
        vt = vt_ref[0, j, 0]
        c = kinv_ref[kinv_base + j * (2 * DIFF_HEADS) + idx] * qinv[idx]
        m_prev = m_sc[idx, :, cs]
        m_new = jnp.maximum(m_prev, mx_ref[idx, :, cs] * c)
        alpha = jnp.exp2(m_prev - m_new)
        p = jnp.exp2(s_ref[idx, :, cs] * c.astype(BF16) - m_new.astype(BF16))
        acc_sc[idx, :, cs] = acc_sc[idx, :, cs] * alpha + jnp.dot(vt, p, preferred_element_type=F32)
        m_sc[idx, :, cs] = m_new

    bufs = ((s_a, mx_a), (s_b, mx_b))

    def step(j, parity):
        for g0 in range(0, len(subs), ATT_GROUP):
            for idx, cs in subs[g0:g0 + ATT_GROUP]:
                accumulate_sub(j, *bufs[parity], idx, cs)
            for idx, cs in subs[g0:g0 + ATT_GROUP]:
                scores_sub(j + 1, *bufs[1 - parity], idx, cs)

    for idx, cs in subs:
        scores_sub(0, *bufs[0], idx, cs)

    def body(i, carry):
        for u in range(unroll):
            step(unroll * i + u, u % 2)
        return carry

    n_loop = (nk - 1) // unroll
    lax.fori_loop(0, n_loop, body, 0)
    for j in range(n_loop * unroll, nk - 1):
        step(j, j % 2)
    for idx, cs in subs:
        accumulate_sub(nk - 1, *bufs[(nk - 1) % 2], idx, cs)

    lam = (jnp.exp(jnp.sum(lq1_ref[...] * lk1_ref[...], axis=-1, keepdims=True))
           - jnp.exp(jnp.sum(lq2_ref[...] * lk2_ref[...], axis=-1, keepdims=True)) + lambda_init)
    o = (acc_sc[0, 0:hd, :] / acc_sc[0, hd:hd + 1, :]
         - lam * (acc_sc[1, 0:hd, :] / acc_sc[1, hd:hd + 1, :]))
    ms = jnp.mean(o * o, axis=0, keepdims=True)
    o = (o * lax.rsqrt(ms + SUBLN_EPS)).T
    o_ref[...] = (o * sub_ref[...] * (1.0 - lambda_init)).astype(BF16)


def _pick_unroll(nk):
    fits = [u for u in range(2, ATT_MAX_CODE_STEPS + 1, 2) if u + (nk - 1) % u <= ATT_MAX_CODE_STEPS]
    return max(fits)


def _attn(qt, kx, kinv, vt, lq1, lk1, lq2, lk2, subln, batch, seq, tq, tk, lambda_init):
    nq = seq // tq
    nk = seq // tk
    hd = 2 * DIFF_DIM
    const = lambda b, h, i: (0, 0)
    return pl.pallas_call(
        functools.partial(_attn_kernel, tk=tk, nk=nk, unroll=_pick_unroll(nk), lambda_init=lambda_init),
        grid=(batch, DIFF_HEADS, nq),
        in_specs=[pl.BlockSpec(memory_space=pltpu.SMEM),
                  pl.BlockSpec((1, hd, tq), lambda b, h, i: (b, h, i)),
                  pl.BlockSpec((seq, 4 * hd), lambda b, h, i: (b, h)),
                  pl.BlockSpec((1, nk, 1, hd + ONES_ROWS, tk), lambda b, h, i: (b, 0, h, 0, 0)),
                  pl.BlockSpec((1, DIFF_DIM), const), pl.BlockSpec((1, DIFF_DIM), const),
                  pl.BlockSpec((1, DIFF_DIM), const), pl.BlockSpec((1, DIFF_DIM), const),
                  pl.BlockSpec((1, hd), const)],
        out_specs=pl.BlockSpec((tq, hd), lambda b, h, i: (b * nq + i, h)),
        out_shape=jax.ShapeDtypeStruct((batch * seq, DIFF_HEADS * hd), BF16),
        scratch_shapes=[pltpu.VMEM((2, 4 * DIFF_DIM, tq), F8),
                        pltpu.VMEM((2, tk, tq), BF16), pltpu.VMEM((2, tk, tq), BF16),
                        pltpu.VMEM((2, 1, tq), F32), pltpu.VMEM((2, 1, tq), F32),
                        pltpu.VMEM((2, 1, tq), F32),
                        pltpu.VMEM((2, hd + ONES_ROWS, tq), F32)],
        compiler_params=pltpu.CompilerParams(dimension_semantics=("arbitrary", "arbitrary", "arbitrary"),
                                             vmem_limit_bytes=VMEM_LIMIT),
        name="attn",
    )(kinv.reshape(-1), qt, kx, vt, lq1, lk1, lq2, lk2, subln)


def _trunk(x, p):
    batch, seq, _ = x.shape
    t = batch * seq
    tm = min(ROW_TILE, seq)
    tq = min(ATT_TQ, seq)
    x2 = x.reshape(t, D_MODEL)

    proj, o_f = _hyb_in(x2, p["norm_mix"][0:1], p["hyb_w_in"], _rope_tables(seq, RET_DIM), p["hyb_decay_fwd"],
                        seq, tm)
    x2 = _hyb_out(x2, proj, o_f, p["hyb_decay_bwd"], p["hyb_conv_w"], p["hyb_gn"], p["hyb_w_out"],
                  batch, seq, tm)
    x2 = _ffn(x2, p["norm_ffn"][0:1], p["ffn_w_gate"][0], p["ffn_w_up"][0], p["ffn_w_down"][0], tm)

    lambda_init = 0.8 - 0.6 * math.exp(-0.3 * 1)
    qt, kx, kinv, vt = _qkv(x2, p["norm_mix"][1:2], p["diff_wq_t"], p["diff_wk"], p["diff_wv_t"],
                            _rope_tables(seq, DIFF_DIM), _rope_tables_t(seq, DIFF_DIM), batch, seq, tm)
    att = _attn(qt, kx, kinv, vt, p["diff_lq1"], p["diff_lk1"], p["diff_lq2"], p["diff_lk2"], p["diff_subln"],
                batch, seq, tq, tm, lambda_init)
    x2 = _ffn(x2, p["norm_ffn"][1:2], p["ffn_w_gate"][1], p["ffn_w_up"][1], p["ffn_w_down"][1], tm,
              pre=(att, p["diff_w_out"]), final_w=p["norm_final"])
    return x2.reshape(batch, seq, D_MODEL)


def kernel(x_prompt, x_sample, norm_mix, norm_ffn, norm_final, hyb_w_in, hyb_conv_w, hyb_decay_fwd,
           hyb_decay_bwd, hyb_gn, hyb_w_out, diff_w_qkv, diff_lq1, diff_lk1, diff_lq2, diff_lk2,
           diff_subln, diff_w_out, ffn_w_gate, ffn_w_up, ffn_w_down):
    assert norm_mix.shape[0] == 2 and hyb_w_in.shape[0] == 1 and diff_w_qkv.shape[0] == 1
    p = {
        "norm_mix": norm_mix, "norm_ffn": norm_ffn, "norm_final": norm_final.reshape(1, D_MODEL),
        "hyb_w_in": hyb_w_in[0].astype(BF16), "hyb_conv_w": hyb_conv_w[0],
        "hyb_decay_fwd": hyb_decay_fwd, "hyb_decay_bwd": hyb_decay_bwd,
        "hyb_gn": hyb_gn, "hyb_w_out": hyb_w_out[0].astype(BF16),
        "diff_wq_t": diff_w_qkv[0][:, :D_MODEL].T.astype(BF16),
        "diff_wk": diff_w_qkv[0][:, D_MODEL:2 * D_MODEL].astype(BF16),
        "diff_wv_t": diff_w_qkv[0][:, 2 * D_MODEL:].T.astype(BF16),
        "diff_lq1": diff_lq1, "diff_lk1": diff_lk1, "diff_lq2": diff_lq2, "diff_lk2": diff_lk2,
        "diff_subln": diff_subln, "diff_w_out": diff_w_out[0].astype(BF16),
        "ffn_w_gate": ffn_w_gate.astype(BF16), "ffn_w_up": ffn_w_up.astype(BF16),
        "ffn_w_down": ffn_w_down.astype(BF16),
    }
    return (_trunk(x_prompt, p), _trunk(x_sample, p))
```

```python
import functools
import math

import jax
import jax.numpy as jnp
from jax import lax
from jax.experimental import pallas as pl
from jax.experimental.pallas import tpu as pltpu

F32 = jnp.float32
BF16 = jnp.bfloat16
F8 = jnp.float8_e4m3fn

D_MODEL = 1024
CONV_CH = 512
RET_HEADS = 4
RET_DIM = 128
RET_WIDTH = RET_HEADS * RET_DIM
RET_CHUNK = 128
DIFF_HEADS = 8
DIFF_DIM = 64
V_DIM = 2 * DIFF_DIM
D_FF = 2816
ROPE_THETA = 10000.0
NORM_EPS = 1e-6
LOG2E = math.log2(math.e)
GN_EPS = 1e-5
SUBLN_EPS = 1e-5
IN_PROJ_COLS = 3 * CONV_CH + 4 * RET_WIDTH

LANES = 128
SUBLANES = 8
COL_CHUNK = 512
MXU_COLS = 256
FF_CHUNK = MXU_COLS
ROW_TILE = 512
ATT_TQ = 2048
ATT_TK = 1024
ATT_MAX_BLOCK_STEPS = 5
VMEM_LIMIT = 60 * 1024 * 1024


def _rms(x, w, eps):
    ms = jnp.mean(x * x, axis=-1, keepdims=True)
    return (x * lax.rsqrt(ms + eps)) * w


def _rope_tables(seq, dim):
    half = dim // 2
    inv = ROPE_THETA ** (-jnp.arange(0, dim, 2, dtype=F32) / dim)
    ang = jnp.arange(seq, dtype=F32)[:, None] * inv[None, :]
    cos = jnp.cos(ang)
    sin = jnp.sin(ang)
    zero = jnp.zeros_like(sin)
    reps = LANES // dim
    cos_t = jnp.tile(jnp.concatenate([cos, cos], axis=1), (1, reps))
    sin_lo = jnp.tile(jnp.concatenate([-sin, zero], axis=1), (1, reps))
    sin_hi = jnp.tile(jnp.concatenate([zero, sin], axis=1), (1, reps))
    return cos_t, sin_lo, sin_hi


def _rope_tables_t(seq, dim):
    inv = ROPE_THETA ** (-jnp.arange(0, dim, 2, dtype=F32) / dim)
    ang = jnp.arange(seq, dtype=F32)[:, None] * inv[None, :]
    return jnp.cos(ang).T, jnp.sin(ang).T


def _rope128(y, cos, sin_lo, sin_hi, half):
    return (y * cos + pltpu.roll(y, LANES - half, 1) * sin_lo + pltpu.roll(y, half, 1) * sin_hi)


def _hyb_in_kernel(x_ref, nw_ref, w_ref, cos_ref, slo_ref, shi_ref, dec_ref, o_ref, of_ref, state_sc, *, spt):
    @pl.when(pl.program_id(0) % spt == 0)
    def _():
        state_sc[...] = jnp.zeros_like(state_sc)

    xn = _rms(x_ref[...], nw_ref[...], NORM_EPS).astype(BF16)
    cos, slo, shi = cos_ref[...], slo_ref[...], shi_ref[...]
    q_chunk = 3 * CONV_CH // COL_CHUNK

    def project(c):
        y = jnp.dot(xn, w_ref[:, c * COL_CHUNK:(c + 1) * COL_CHUNK], preferred_element_type=F32)
        if c in (q_chunk, q_chunk + 1):
            scale = 1.0 if c == q_chunk else RET_DIM ** -0.5
            for g in range(COL_CHUNK // LANES):
                r = _rope128(y[:, g * LANES:(g + 1) * LANES], cos, slo, shi, RET_DIM // 2)
                if scale != 1.0:
                    r = r * scale
                o_ref[:, c * COL_CHUNK + g * LANES:c * COL_CHUNK + (g + 1) * LANES] = r.astype(BF16)
        else:
            o_ref[:, c * COL_CHUNK:(c + 1) * COL_CHUNK] = y.astype(BF16)

    for c in (q_chunk, q_chunk + 1, q_chunk + 2):
        project(c)

    tabs = [_decay_tables(-jnp.exp(dec_ref[:, h:h + 1]), strict=False) for h in range(RET_HEADS)]
    qb, kb, vb = (c * COL_CHUNK for c in (q_chunk, q_chunk + 1, q_chunk + 2))
    for c in range(x_ref.shape[0] // RET_CHUNK):
        rows = slice(c * RET_CHUNK, (c + 1) * RET_CHUNK)
        for h in range(RET_HEADS):
            lo = h * RET_DIM
            o, st = _ret_chunk(o_ref[rows, qb + lo:qb + lo + RET_DIM], o_ref[rows, kb + lo:kb + lo + RET_DIM],
                               o_ref[rows, vb + lo:vb + lo + RET_DIM], state_sc[h], *tabs[h])
            of_ref[rows, lo:lo + RET_DIM] = o
            state_sc[h] = st

    for c in range(IN_PROJ_COLS // COL_CHUNK):
        if c not in (q_chunk, q_chunk + 1, q_chunk + 2):
            project(c)


def _hyb_in(x2, nw, w_in, tabs, dec, seq, tm):
    t = x2.shape[0]
    spt = seq // tm
    row = lambda i: (i, 0)
    pos = lambda i: (i % spt, 0)
    const = lambda i: (0, 0)
    return pl.pallas_call(
        functools.partial(_hyb_in_kernel, spt=spt),
        grid=(t // tm,),
        in_specs=[pl.BlockSpec((tm, D_MODEL), row),
                  pl.BlockSpec((1, D_MODEL), const),
                  pl.BlockSpec((D_MODEL, IN_PROJ_COLS), const),
                  pl.BlockSpec((tm, LANES), pos),
                  pl.BlockSpec((tm, LANES), pos),
                  pl.BlockSpec((tm, LANES), pos),
                  pl.BlockSpec((1, RET_HEADS), const)],
        out_specs=[pl.BlockSpec((tm, IN_PROJ_COLS), row),
                   pl.BlockSpec((tm, RET_WIDTH), row)],
        out_shape=[jax.ShapeDtypeStruct((t, IN_PROJ_COLS), BF16),
                   jax.ShapeDtypeStruct((t, RET_WIDTH), F32)],
        scratch_shapes=[pltpu.VMEM((RET_HEADS, RET_DIM, RET_DIM), F32)],
        compiler_params=pltpu.CompilerParams(dimension_semantics=("arbitrary",),
                                             vmem_limit_bytes=VMEM_LIMIT),
        name="hyb_in",
    )(x2, nw, w_in, *tabs, dec)


def _decay_tables(lg, strict):
    c = RET_CHUNK
    ii = lax.broadcasted_iota(jnp.int32, (c, c), 0).astype(F32)
    jj = lax.broadcasted_iota(jnp.int32, (c, c), 1).astype(F32)
    if strict:
        rel = jj - ii
        mask = rel > 0
        kdec = jnp.exp(lg * ii)
        qdec = jnp.exp(lg * (c - ii))
    else:
        rel = ii - jj
        mask = rel >= 0
        kdec = jnp.exp(lg * (c - 1.0 - ii))
        qdec = jnp.exp(lg * (ii + 1.0))
    dintra = jnp.where(mask, jnp.exp(lg * jnp.maximum(rel, 0.0)), 0.0)
    return dintra, kdec, qdec, jnp.exp(lg * c)


def _ret_chunk(q, k, v, st, dintra, kdec, qdec, cdec):
    sc = lax.dot_general(q, k, (((1,), (1,)), ((), ())), preferred_element_type=F32) * dintra
    o = jnp.dot(sc.astype(BF16), v, preferred_element_type=F32)
    o = o + jnp.dot(q, st.astype(BF16), preferred_element_type=F32) * qdec
    kd_t = (k.astype(F32) * kdec).T.astype(BF16)
    kv = jnp.dot(kd_t, v, preferred_element_type=F32)
    return o, st * cdec + kv


def _hyb_out_kernel(x_ref, ab_ref, ac_ref, ah_ref, q_ref, k_ref, v_ref, g_ref, of_ref,
                    acp_ref, ahp_ref, acn_ref, ahn_ref, dec_ref, cw_ref, gn_ref, wo_ref,
                    o_ref, state_sc, y_sc, *, n_chunks, nsteps):
    step = pl.program_id(1)

    @pl.when(step == 0)
    def _():
        state_sc[...] = jnp.zeros_like(state_sc)

    tabs = [_decay_tables(-jnp.exp(dec_ref[:, h:h + 1]), strict=True) for h in range(RET_HEADS)]
    for c in reversed(range(n_chunks)):
        rows = slice(c * RET_CHUNK, (c + 1) * RET_CHUNK)
        for h in range(RET_HEADS):
            cols = slice(h * RET_DIM, (h + 1) * RET_DIM)
            gn_w = gn_ref[:, cols]
            o, st = _ret_chunk(q_ref[rows, cols], k_ref[rows, cols], v_ref[rows, cols], state_sc[h], *tabs[h])
            state_sc[h] = st
            o = o + of_ref[rows, cols]
            mu = jnp.mean(o, axis=-1, keepdims=True)
            d = o - mu
            var = jnp.mean(d * d, axis=-1, keepdims=True)
            on = d * lax.rsqrt(var + GN_EPS) * gn_w
            g = g_ref[rows, cols].astype(F32)
            y_sc[rows, CONV_CH + h * RET_DIM:CONV_CH + (h + 1) * RET_DIM] = (jax.nn.silu(g) * on).astype(BF16)

    tc = n_chunks * RET_CHUNK
    tile = nsteps - 1 - step
    u = ac_ref[...].astype(F32) * ah_ref[...].astype(F32)
    u_prev = acp_ref[SUBLANES - 1:SUBLANES, :].astype(F32) * ahp_ref[SUBLANES - 1:SUBLANES, :].astype(F32)
    u_next = acn_ref[0:1, :].astype(F32) * ahn_ref[0:1, :].astype(F32)
    u_prev = jnp.where(tile > 0, u_prev, 0.0)
    u_next = jnp.where(tile < nsteps - 1, u_next, 0.0)
    ridx = lax.broadcasted_iota(jnp.int32, u.shape, 0)
    up = jnp.where(ridx == 0, u_prev, pltpu.roll(u, 1, 0))
    un = jnp.where(ridx == tc - 1, u_next, pltpu.roll(u, tc - 1, 0))
    conv = cw_ref[0:1, :] * up + cw_ref[1:2, :] * u + cw_ref[2:3, :] * un
    y_sc[:, 0:CONV_CH] = (ab_ref[...].astype(F32) * conv).astype(BF16)

    o_ref[...] = x_ref[...] + jnp.dot(y_sc[...], wo_ref[...], preferred_element_type=F32)


def _hyb_out(x2, proj, o_f, dec, conv_w, gn_w, w_out, batch, seq, tc):
    t = x2.shape[0]
    nsteps = seq // tc
    rpt = tc // SUBLANES
    n8 = t // SUBLANES
    tile = lambda b, i: b * nsteps + (nsteps - 1 - i)

    def blk(col, width=RET_WIDTH):
        return pl.BlockSpec((tc, width), lambda b, i: (tile(b, i), col))

    def halo_prev(col):
        return pl.BlockSpec((SUBLANES, CONV_CH), lambda b, i: (jnp.maximum(tile(b, i) * rpt - 1, 0), col))

    def halo_next(col):
        return pl.BlockSpec((SUBLANES, CONV_CH), lambda b, i: (jnp.minimum((tile(b, i) + 1) * rpt, n8 - 1), col))

    const = lambda b, i: (0, 0)
    return pl.pallas_call(
        functools.partial(_hyb_out_kernel, n_chunks=tc // RET_CHUNK, nsteps=nsteps),
        grid=(batch, nsteps),
        in_specs=[blk(0, D_MODEL),
                  blk(0), blk(1), blk(2), blk(3), blk(4), blk(5), blk(6), blk(0),
                  halo_prev(1), halo_prev(2), halo_next(1), halo_next(2),
                  pl.BlockSpec((1, RET_HEADS), const),
                  pl.BlockSpec((3, CONV_CH), const),
                  pl.BlockSpec((1, RET_WIDTH), const),
                  pl.BlockSpec((CONV_CH + RET_WIDTH, D_MODEL), const)],
        out_specs=blk(0, D_MODEL),
        out_shape=jax.ShapeDtypeStruct((t, D_MODEL), F32),
        scratch_shapes=[pltpu.VMEM((RET_HEADS, RET_DIM, RET_DIM), F32),
                        pltpu.VMEM((tc, CONV_CH + RET_WIDTH), BF16)],
        compiler_params=pltpu.CompilerParams(dimension_semantics=("arbitrary", "arbitrary"),
                                             vmem_limit_bytes=VMEM_LIMIT),
        name="hyb_out",
    )(x2, proj, proj, proj, proj, proj, proj, proj, o_f, proj, proj, proj, proj, dec, conv_w, gn_w, w_out)


def _ffn_kernel(*refs, pre_proj, final_norm):
    refs = list(refs)
    x_ref = refs.pop(0)
    if pre_proj:
        a_ref, wo_ref = refs.pop(0), refs.pop(0)
    nw_ref, wg_ref, wu_ref, wd_ref = refs[:4]
    refs = refs[4:]
    if final_norm:
        fw_ref = refs.pop(0)
    o_ref = refs.pop(0)

    x = x_ref[...]
    if pre_proj:
        x = x + jnp.dot(a_ref[...], wo_ref[...], preferred_element_type=F32)
    xn = _rms(x, nw_ref[...], NORM_EPS).astype(BF16)
    acc = x
    for c in range(D_FF // FF_CHUNK):
        cs = slice(c * FF_CHUNK, (c + 1) * FF_CHUNK)
        g = jnp.dot(xn, wg_ref[:, cs], preferred_element_type=F32)
        u = jnp.dot(xn, wu_ref[:, cs], preferred_element_type=F32)
        h = (jax.nn.silu(g) * u).astype(BF16)
        acc = acc + jnp.dot(h, wd_ref[cs, :], preferred_element_type=F32)
    if final_norm:
        acc = _rms(acc, fw_ref[...], NORM_EPS)
    o_ref[...] = acc


def _ffn(x2, nw, wg, wu, wd, tm, pre=None, final_w=None):
    t = x2.shape[0]
    row = lambda i: (i, 0)
    const = lambda i: (0, 0)
    single = pl.Buffered(1)
    args = [x2]
    specs = [pl.BlockSpec((tm, D_MODEL), row)]
    if pre is not None:
        a, wo = pre
        args += [a, wo]
        specs += [pl.BlockSpec((tm, D_MODEL), row),
                  pl.BlockSpec((D_MODEL, D_MODEL), const, pipeline_mode=single)]
    args += [nw, wg, wu, wd]
    specs += [pl.BlockSpec((1, D_MODEL), const),
              pl.BlockSpec((D_MODEL, D_FF), const, pipeline_mode=single),
              pl.BlockSpec((D_MODEL, D_FF), const, pipeline_mode=single),
              pl.BlockSpec((D_FF, D_MODEL), const, pipeline_mode=single)]
    if final_w is not None:
        args.append(final_w)
        specs.append(pl.BlockSpec((1, D_MODEL), const))
    return pl.pallas_call(
        functools.partial(_ffn_kernel, pre_proj=pre is not None, final_norm=final_w is not None),
        grid=(t // tm,),
        in_specs=specs,
        out_specs=pl.BlockSpec((tm, D_MODEL), row),
        out_shape=jax.ShapeDtypeStruct((t, D_MODEL), F32),
        compiler_params=pltpu.CompilerParams(dimension_semantics=("arbitrary",),
                                             vmem_limit_bytes=VMEM_LIMIT),
        name="ffn",
    )(*args)


FP8_TOP_EXP = 7
FP8_EXP_CLIP = 60


def _pow2_scale(m):
    exp = (lax.bitcast_convert_type(m, jnp.int32) >> 23) - 127
    es = jnp.clip(FP8_TOP_EXP - exp, -FP8_EXP_CLIP, FP8_EXP_CLIP)
    scale = lax.bitcast_convert_type((es + 127) << 23, F32)
    inv = lax.bitcast_convert_type((127 - es) << 23, F32)
    return scale, inv


def _fp8_split(x):
    hi = x.astype(F8).astype(F32)
    return hi, x - hi


def _absmax(x):
    return jnp.max(jnp.max(x, axis=0, keepdims=True), axis=1, keepdims=True)


def _qkv_kernel(x_ref, nw_ref, wqt_ref, wk_ref, wvt_ref, cos_ref, slo_ref, shi_ref, cost_ref, sint_ref,
                qt_ref, kx_ref, kinv_ref, vt_ref):
    xn = _rms(x_ref[...], nw_ref[...], NORM_EPS).astype(BF16)
    xnt = xn.T
    half = DIFF_DIM // 2

    cost, sint = cost_ref[...], sint_ref[...]
    for c in range(D_MODEL // COL_CHUNK):
        y = jnp.dot(wqt_ref[c * COL_CHUNK:(c + 1) * COL_CHUNK, :], xnt, preferred_element_type=F32)
        for hrow in range(0, COL_CHUNK, DIFF_DIM):
            x1, x2 = y[hrow:hrow + half], y[hrow + half:hrow + DIFF_DIM]
            lo = c * COL_CHUNK + hrow
            qscale = DIFF_DIM ** -0.5 * LOG2E
            qt_ref[0, lo:lo + half, :] = ((x1 * cost - x2 * sint) * qscale).astype(BF16)
            qt_ref[0, lo + half:lo + DIFF_DIM, :] = ((x2 * cost + x1 * sint) * qscale).astype(BF16)

    cos, slo, shi = cos_ref[...], slo_ref[...], shi_ref[...]
    head_lane = lax.broadcasted_iota(jnp.int32, (1, 2 * DIFF_HEADS), 1)
    kinv = jnp.zeros((1, 2 * DIFF_HEADS), F32)
    for c in range(D_MODEL // COL_CHUNK):
        y = jnp.dot(xn, wk_ref[:, c * COL_CHUNK:(c + 1) * COL_CHUNK], preferred_element_type=F32)
        for g in range(COL_CHUNK // LANES):
            r = _rope128(y[:, g * LANES:(g + 1) * LANES], cos, slo, shi, half)
            first = lax.broadcasted_iota(jnp.int32, r.shape, 1) < DIFF_DIM
            colmax = jnp.max(jnp.abs(r), axis=0, keepdims=True)
            sc1, inv1 = _pow2_scale(jnp.max(jnp.where(first[0:1], colmax, 0.0), axis=1, keepdims=True))
            sc2, inv2 = _pow2_scale(jnp.max(jnp.where(first[0:1], 0.0, colmax), axis=1, keepdims=True))
            hi, lw = _fp8_split(r * jnp.where(first[0:1], sc1, sc2))
            g1 = jnp.where(first, hi, pltpu.roll(lw, DIFF_DIM, 1)).astype(F8)
            g2 = jnp.where(first, pltpu.roll(hi, DIFF_DIM, 1), lw).astype(F8)
            pair = c * (COL_CHUNK // LANES) + g
            base = pair * 4 * LANES
            kx_ref[:, base:base + LANES] = g1
            kx_ref[:, base + LANES:base + 2 * LANES] = g1
            kx_ref[:, base + 2 * LANES:base + 3 * LANES] = g2
            kx_ref[:, base + 3 * LANES:base + 4 * LANES] = g2
            kinv = jnp.where(head_lane == 2 * pair, inv1, kinv)
            kinv = jnp.where(head_lane == 2 * pair + 1, inv2, kinv)
    kinv_ref[0] = kinv

    for c in range(D_MODEL // COL_CHUNK):
        y = jnp.dot(wvt_ref[c * COL_CHUNK:(c + 1) * COL_CHUNK, :], xnt, preferred_element_type=F32)
        for r0 in range(0, COL_CHUNK, V_DIM):
            head = (c * COL_CHUNK + r0) // V_DIM
            vt_ref[0, 0, head, 0:V_DIM, :] = y[r0:r0 + V_DIM].astype(BF16)
            vt_ref[0, 0, head, V_DIM:V_DIM + ONES_ROWS, :] = jnp.ones((ONES_ROWS, x_ref.shape[0]), BF16)


def _qkv(x2, nw, wq_t, wk, wv_t, tabs, tabs_t, batch, seq, tm):
    t = x2.shape[0]
    spt = seq // tm
    row = lambda i: (i, 0)
    pos = lambda i: (i % spt, 0)
    pos_t = lambda i: (0, i % spt)
    const = lambda i: (0, 0)
    sq = (D_MODEL, D_MODEL)
    return pl.pallas_call(
        _qkv_kernel,
        grid=(t // tm,),
        in_specs=[pl.BlockSpec((tm, D_MODEL), row),
                  pl.BlockSpec((1, D_MODEL), const),
                  pl.BlockSpec(sq, const), pl.BlockSpec(sq, const), pl.BlockSpec(sq, const),
                  pl.BlockSpec((tm, LANES), pos),
                  pl.BlockSpec((tm, LANES), pos),
                  pl.BlockSpec((tm, LANES), pos),
                  pl.BlockSpec((DIFF_DIM // 2, tm), pos_t),
                  pl.BlockSpec((DIFF_DIM // 2, tm), pos_t)],
        out_specs=[pl.BlockSpec((1, D_MODEL, tm), lambda i: (i // spt, 0, i % spt)),
                   pl.BlockSpec((tm, 4 * D_MODEL), row),
                   pl.BlockSpec((1, 1, 2 * DIFF_HEADS), lambda i: (i, 0, 0)),
                   pl.BlockSpec((1, 1, DIFF_HEADS, V_DIM + ONES_ROWS, tm), lambda i: (i // spt, i % spt, 0, 0, 0))],
        out_shape=[jax.ShapeDtypeStruct((batch, D_MODEL, seq), BF16),
                   jax.ShapeDtypeStruct((t, 4 * D_MODEL), F8),
                   jax.ShapeDtypeStruct((t // tm, 1, 2 * DIFF_HEADS), F32),
                   jax.ShapeDtypeStruct((batch, spt, DIFF_HEADS, V_DIM + ONES_ROWS, tm), BF16)],
        compiler_params=pltpu.CompilerParams(dimension_semantics=("arbitrary",),
                                             vmem_limit_bytes=VMEM_LIMIT),
        name="qkv",
    )(x2, nw, wq_t, wk, wv_t, *tabs, *tabs_t)


ONES_ROWS = 16


def _attn_kernel(kinv_ref, qt_ref, kx_ref, vt_ref, lq1_ref, lk1_ref, lq2_ref, lk2_ref, sub_ref, o_ref,
                 qx_sc, s_a, s_b, mx_a, mx_b, m_sc, acc_sc, *, tk, nk, unroll, lambda_init):
    hd = 2 * DIFF_DIM
    qt = qt_ref[0].astype(F32)
    qinv = []
    for idx in range(2):
        qh = qt[idx * DIFF_DIM:(idx + 1) * DIFF_DIM]
        scale, inv = _pow2_scale(_absmax(jnp.abs(qh)))
        hi, lw = _fp8_split(qh * scale)
        qx_sc[idx] = jnp.concatenate([hi, hi, lw, lw], axis=0).astype(F8)
        qinv.append(inv)
    m_sc[...] = jnp.full_like(m_sc, -jnp.inf)
    acc_sc[...] = jnp.zeros_like(acc_sc)
    kinv_base = (pl.program_id(0) * nk) * (2 * DIFF_HEADS) + 2 * pl.program_id(1)

    subs = [(idx, slice(c0, c0 + MXU_COLS)) for idx in range(2) for c0 in range(0, qt.shape[1], MXU_COLS)]

    def scores_sub(j, s_ref, mx_ref, idx, cs):
        off = pl.multiple_of(j * tk, tk)
        kx = kx_ref[pl.ds(off, tk), idx * 4 * DIFF_DIM:(idx + 1) * 4 * DIFF_DIM]
        st = jnp.dot(kx, qx_sc[idx, :, cs], preferred_element_type=F32).astype(BF16)
        s_ref[idx, :, cs] = st
        mx_ref[idx, :, cs] = jnp.max(st, axis=0, keepdims=True).astype(F32)

    def accumulate_sub(j, s_ref, mx_ref, idx, cs):
        vt = vt_ref[0, j, 0]
        c = kinv_ref[kinv_base + j * (2 * DIFF_HEADS) + idx] * qinv[idx]
        m_prev = m_sc[idx, :, cs]
        m_new = jnp.maximum(m_prev, mx_ref[idx, :, cs] * c)
        alpha = jnp.exp2(m_prev - m_new)
        p = jnp.exp2(s_ref[idx, :, cs] * c.astype(BF16) - m_new.astype(BF16))
        acc_sc[idx, :, cs] = acc_sc[idx, :, cs] * alpha + jnp.dot(vt, p, preferred_element_type=F32)
        m_sc[idx, :, cs] = m_new

    bufs = ((s_a, mx_a), (s_b, mx_b))

    def step(j, parity):
        for idx, cs in subs:
            accumulate_sub(j, *bufs[parity], idx, cs)
            scores_sub(j + 1, *bufs[1 - parity], idx, cs)

    for idx, cs in subs:
        scores_sub(0, *bufs[0], idx, cs)

    def body(i, carry):
        for u in range(unroll):
            step(unroll * i + u, u % 2)
        return carry

    n_loop = (nk - 1) // unroll
    lax.fori_loop(0, n_loop, body, 0)
    for j in range(n_loop * unroll, nk - 1):
        step(j, j % 2)
    for idx, cs in subs:
        accumulate_sub(nk - 1, *bufs[(nk - 1) % 2], idx, cs)

    lam = (jnp.exp(jnp.sum(lq1_ref[...] * lk1_ref[...], axis=-1, keepdims=True))
           - jnp.exp(jnp.sum(lq2_ref[...] * lk2_ref[...], axis=-1, keepdims=True)) + lambda_init)
    o = (acc_sc[0, 0:hd, :] / acc_sc[0, hd:hd + 1, :]
         - lam * (acc_sc[1, 0:hd, :] / acc_sc[1, hd:hd + 1, :]))
    ms = jnp.mean(o * o, axis=0, keepdims=True)
    o = (o * lax.rsqrt(ms + SUBLN_EPS)).T
    o_ref[...] = (o * sub_ref[...] * (1.0 - lambda_init)).astype(BF16)


def _pick_unroll(nk):
    steps = nk - 1
    fits = [u for u in range(2, ATT_MAX_BLOCK_STEPS + 1, 2)
            if steps // u >= 2 and steps % u + 1 <= ATT_MAX_BLOCK_STEPS]
    return max(fits, default=2)


def _attn(qt, kx, kinv, vt, lq1, lk1, lq2, lk2, subln, batch, seq, tq, tk, lambda_init):
    nq = seq // tq
    nk = seq // tk
    hd = 2 * DIFF_DIM
    const = lambda b, h, i: (0, 0)
    return pl.pallas_call(
        functools.partial(_attn_kernel, tk=tk, nk=nk, unroll=_pick_unroll(nk), lambda_init=lambda_init),
        grid=(batch, DIFF_HEADS, nq),
        in_specs=[pl.BlockSpec(memory_space=pltpu.SMEM),
                  pl.BlockSpec((1, hd, tq), lambda b, h, i: (b, h, i)),
                  pl.BlockSpec((seq, 4 * hd), lambda b, h, i: (b, h)),
                  pl.BlockSpec((1, nk, 1, hd + ONES_ROWS, tk), lambda b, h, i: (b, 0, h, 0, 0)),
                  pl.BlockSpec((1, DIFF_DIM), const), pl.BlockSpec((1, DIFF_DIM), const),
                  pl.BlockSpec((1, DIFF_DIM), const), pl.BlockSpec((1, DIFF_DIM), const),
                  pl.BlockSpec((1, hd), const)],
        out_specs=pl.BlockSpec((tq, hd), lambda b, h, i: (b * nq + i, h)),
        out_shape=jax.ShapeDtypeStruct((batch * seq, DIFF_HEADS * hd), BF16),
        scratch_shapes=[pltpu.VMEM((2, 4 * DIFF_DIM, tq), F8),
                        pltpu.VMEM((2, tk, tq), BF16), pltpu.VMEM((2, tk, tq), BF16),
                        pltpu.VMEM((2, 1, tq), F32), pltpu.VMEM((2, 1, tq), F32),
                        pltpu.VMEM((2, 1, tq), F32),
                        pltpu.VMEM((2, hd + ONES_ROWS, tq), F32)],
        compiler_params=pltpu.CompilerParams(dimension_semantics=("arbitrary", "arbitrary", "arbitrary"),
                                             vmem_limit_bytes=VMEM_LIMIT),
        name="attn",
    )(kinv.reshape(-1), qt, kx, vt, lq1, lk1, lq2, lk2, subln)


def _trunk(x, p):
    batch, seq, _ = x.shape
    t = batch * seq
    tm = min(ROW_TILE, seq)
    tq = min(ATT_TQ, seq)
    tk = min(ATT_TK, seq)
    x2 = x.reshape(t, D_MODEL)

    proj, o_f = _hyb_in(x2, p["norm_mix"][0:1], p["hyb_w_in"], _rope_tables(seq, RET_DIM), p["hyb_decay_fwd"],
                        seq, tm)
    x2 = _hyb_out(x2, proj, o_f, p["hyb_decay_bwd"], p["hyb_conv_w"], p["hyb_gn"], p["hyb_w_out"],
                  batch, seq, tm)
    x2 = _ffn(x2, p["norm_ffn"][0:1], p["ffn_w_gate"][0], p["ffn_w_up"][0], p["ffn_w_down"][0], tm)

    lambda_init = 0.8 - 0.6 * math.exp(-0.3 * 1)
    qt, kx, kinv, vt = _qkv(x2, p["norm_mix"][1:2], p["diff_wq_t"], p["diff_wk"], p["diff_wv_t"],
                            _rope_tables(seq, DIFF_DIM), _rope_tables_t(seq, DIFF_DIM), batch, seq, tk)
    att = _attn(qt, kx, kinv, vt, p["diff_lq1"], p["diff_lk1"], p["diff_lq2"], p["diff_lk2"], p["diff_subln"],
                batch, seq, tq, tk, lambda_init)
    x2 = _ffn(x2, p["norm_ffn"][1:2], p["ffn_w_gate"][1], p["ffn_w_up"][1], p["ffn_w_down"][1], tm,
              pre=(att, p["diff_w_out"]), final_w=p["norm_final"])
    return x2.reshape(batch, seq, D_MODEL)


def kernel(x_prompt, x_sample, norm_mix, norm_ffn, norm_final, hyb_w_in, hyb_conv_w, hyb_decay_fwd,
           hyb_decay_bwd, hyb_gn, hyb_w_out, diff_w_qkv, diff_lq1, diff_lk1, diff_lq2, diff_lk2,
           diff_subln, diff_w_out, ffn_w_gate, ffn_w_up, ffn_w_down):
    assert norm_mix.shape[0] == 2 and hyb_w_in.shape[0] == 1 and diff_w_qkv.shape[0] == 1
    p = {
        "norm_mix": norm_mix, "norm_ffn": norm_ffn, "norm_final": norm_final.reshape(1, D_MODEL),
        "hyb_w_in": hyb_w_in[0].astype(BF16), "hyb_conv_w": hyb_conv_w[0],
        "hyb_decay_fwd": hyb_decay_fwd, "hyb_decay_bwd": hyb_decay_bwd,
        "hyb_gn": hyb_gn, "hyb_w_out": hyb_w_out[0].astype(BF16),
        "diff_wq_t": diff_w_qkv[0][:, :D_MODEL].T.astype(BF16),
        "diff_wk": diff_w_qkv[0][:, D_MODEL:2 * D_MODEL].astype(BF16),
        "diff_wv_t": diff_w_qkv[0][:, 2 * D_MODEL:].T.astype(BF16),
        "diff_lq1": diff_lq1, "diff_lk1": diff_lk1, "diff_lq2": diff_lq2, "diff_lk2": diff_lk2,
        "diff_subln": diff_subln, "diff_w_out": diff_w_out[0].astype(BF16),
        "ffn_w_gate": ffn_w_gate.astype(BF16), "ffn_w_up": ffn_w_up.astype(BF16),
        "ffn_w_down": ffn_w_down.astype(BF16),
    }
    return (_trunk(x_prompt, p), _trunk(x_sample, p))
```

```python
import functools
import math

import jax
import jax.numpy as jnp
from jax import lax
from jax.experimental import pallas as pl
from jax.experimental.pallas import tpu as pltpu

F32 = jnp.float32
BF16 = jnp.bfloat16
F8 = jnp.float8_e4m3fn

D_MODEL = 1024
CONV_CH = 512
RET_HEADS = 4
RET_DIM = 128
RET_WIDTH = RET_HEADS * RET_DIM
RET_CHUNK = 128
DIFF_HEADS = 8
DIFF_DIM = 64
V_DIM = 2 * DIFF_DIM
D_FF = 2816
ROPE_THETA = 10000.0
NORM_EPS = 1e-6
LOG2E = math.log2(math.e)
GN_EPS = 1e-5
SUBLN_EPS = 1e-5
IN_PROJ_COLS = 3 * CONV_CH + 4 * RET_WIDTH

LANES = 128
SUBLANES = 8
COL_CHUNK = 512
MXU_COLS = 256
FF_CHUNK = MXU_COLS
ROW_TILE = 512
ATT_TQ = 2048
ATT_TK = 1024
ATT_MAX_BLOCK_STEPS = 5
VMEM_LIMIT = 60 * 1024 * 1024


def _rms(x, w, eps):
    ms = jnp.mean(x * x, axis=-1, keepdims=True)
    return (x * lax.rsqrt(ms + eps)) * w


def _rope_tables(seq, dim):
    half = dim // 2
    inv = ROPE_THETA ** (-jnp.arange(0, dim, 2, dtype=F32) / dim)
    ang = jnp.arange(seq, dtype=F32)[:, None] * inv[None, :]
    cos = jnp.cos(ang)
    sin = jnp.sin(ang)
    zero = jnp.zeros_like(sin)
    reps = LANES // dim
    cos_t = jnp.tile(jnp.concatenate([cos, cos], axis=1), (1, reps))
    sin_lo = jnp.tile(jnp.concatenate([-sin, zero], axis=1), (1, reps))
    sin_hi = jnp.tile(jnp.concatenate([zero, sin], axis=1), (1, reps))
    return cos_t, sin_lo, sin_hi


def _rope_tables_t(seq, dim):
    inv = ROPE_THETA ** (-jnp.arange(0, dim, 2, dtype=F32) / dim)
    ang = jnp.arange(seq, dtype=F32)[:, None] * inv[None, :]
    return jnp.cos(ang).T, jnp.sin(ang).T


def _rope128(y, cos, sin_lo, sin_hi, half):
    return (y * cos + pltpu.roll(y, LANES - half, 1) * sin_lo + pltpu.roll(y, half, 1) * sin_hi)


def _hyb_in_kernel(x_ref, nw_ref, w_ref, cos_ref, slo_ref, shi_ref, dec_ref, o_ref, of_ref, state_sc, *, spt):
    @pl.when(pl.program_id(0) % spt == 0)
    def _():
        state_sc[...] = jnp.zeros_like(state_sc)

    xn = _rms(x_ref[...], nw_ref[...], NORM_EPS).astype(BF16)
    cos, slo, shi = cos_ref[...], slo_ref[...], shi_ref[...]
    q_chunk = 3 * CONV_CH // COL_CHUNK

    def project(c):
        y = jnp.dot(xn, w_ref[:, c * COL_CHUNK:(c + 1) * COL_CHUNK], preferred_element_type=F32)
        if c in (q_chunk, q_chunk + 1):
            scale = 1.0 if c == q_chunk else RET_DIM ** -0.5
            for g in range(COL_CHUNK // LANES):
                r = _rope128(y[:, g * LANES:(g + 1) * LANES], cos, slo, shi, RET_DIM // 2)
                if scale != 1.0:
                    r = r * scale
                o_ref[:, c * COL_CHUNK + g * LANES:c * COL_CHUNK + (g + 1) * LANES] = r.astype(BF16)
        else:
            o_ref[:, c * COL_CHUNK:(c + 1) * COL_CHUNK] = y.astype(BF16)

    for c in (q_chunk, q_chunk + 1, q_chunk + 2):
        project(c)

    tabs = [_decay_tables(-jnp.exp(dec_ref[:, h:h + 1]), strict=False) for h in range(RET_HEADS)]
    qb, kb, vb = (c * COL_CHUNK for c in (q_chunk, q_chunk + 1, q_chunk + 2))
    for c in range(x_ref.shape[0] // RET_CHUNK):
        rows = slice(c * RET_CHUNK, (c + 1) * RET_CHUNK)
        for h in range(RET_HEADS):
            lo = h * RET_DIM
            o, st = _ret_chunk(o_ref[rows, qb + lo:qb + lo + RET_DIM], o_ref[rows, kb + lo:kb + lo + RET_DIM],
                               o_ref[rows, vb + lo:vb + lo + RET_DIM], state_sc[h], *tabs[h])
            of_ref[rows, lo:lo + RET_DIM] = o
            state_sc[h] = st

    for c in range(IN_PROJ_COLS // COL_CHUNK):
        if c not in (q_chunk, q_chunk + 1, q_chunk + 2):
            project(c)


def _hyb_in(x2, nw, w_in, tabs, dec, seq, tm):
    t = x2.shape[0]
    spt = seq // tm
    row = lambda i: (i, 0)
    pos = lambda i: (i % spt, 0)
    const = lambda i: (0, 0)
    return pl.pallas_call(
        functools.partial(_hyb_in_kernel, spt=spt),
        grid=(t // tm,),
        in_specs=[pl.BlockSpec((tm, D_MODEL), row),
                  pl.BlockSpec((1, D_MODEL), const),
                  pl.BlockSpec((D_MODEL, IN_PROJ_COLS), const),
                  pl.BlockSpec((tm, LANES), pos),
                  pl.BlockSpec((tm, LANES), pos),
                  pl.BlockSpec((tm, LANES), pos),
                  pl.BlockSpec((1, RET_HEADS), const)],
        out_specs=[pl.BlockSpec((tm, IN_PROJ_COLS), row),
                   pl.BlockSpec((tm, RET_WIDTH), row)],
        out_shape=[jax.ShapeDtypeStruct((t, IN_PROJ_COLS), BF16),
                   jax.ShapeDtypeStruct((t, RET_WIDTH), F32)],
        scratch_shapes=[pltpu.VMEM((RET_HEADS, RET_DIM, RET_DIM), F32)],
        compiler_params=pltpu.CompilerParams(dimension_semantics=("arbitrary",),
                                             vmem_limit_bytes=VMEM_LIMIT),
        name="hyb_in",
    )(x2, nw, w_in, *tabs, dec)


def _decay_tables(lg, strict):
    c = RET_CHUNK
    ii = lax.broadcasted_iota(jnp.int32, (c, c), 0).astype(F32)
    jj = lax.broadcasted_iota(jnp.int32, (c, c), 1).astype(F32)
    if strict:
        rel = jj - ii
        mask = rel > 0
        kdec = jnp.exp(lg * ii)
        qdec = jnp.exp(lg * (c - ii))
    else:
        rel = ii - jj
        mask = rel >= 0
        kdec = jnp.exp(lg * (c - 1.0 - ii))
        qdec = jnp.exp(lg * (ii + 1.0))
    dintra = jnp.where(mask, jnp.exp(lg * jnp.maximum(rel, 0.0)), 0.0)
    return dintra, kdec, qdec, jnp.exp(lg * c)


def _ret_chunk(q, k, v, st, dintra, kdec, qdec, cdec):
    sc = lax.dot_general(q, k, (((1,), (1,)), ((), ())), preferred_element_type=F32) * dintra
    lhs = jnp.concatenate([sc.astype(BF16), (q.astype(F32) * qdec).astype(BF16)], axis=1)
    o = jnp.dot(lhs, jnp.concatenate([v, st.astype(BF16)], axis=0), preferred_element_type=F32)
    kd_t = (k.astype(F32) * kdec).T.astype(BF16)
    kv = jnp.dot(kd_t, v, preferred_element_type=F32)
    return o, st * cdec + kv


def _hyb_out_kernel(x_ref, ab_ref, ac_ref, ah_ref, q_ref, k_ref, v_ref, g_ref, of_ref,
                    acp_ref, ahp_ref, acn_ref, ahn_ref, dec_ref, cw_ref, gn_ref, wo_ref,
                    o_ref, state_sc, y_sc, *, n_chunks, nsteps):
    step = pl.program_id(1)

    @pl.when(step == 0)
    def _():
        state_sc[...] = jnp.zeros_like(state_sc)

    tabs = [_decay_tables(-jnp.exp(dec_ref[:, h:h + 1]), strict=True) for h in range(RET_HEADS)]
    for c in reversed(range(n_chunks)):
        rows = slice(c * RET_CHUNK, (c + 1) * RET_CHUNK)
        for h in range(RET_HEADS):
            cols = slice(h * RET_DIM, (h + 1) * RET_DIM)
            gn_w = gn_ref[:, cols]
            o, st = _ret_chunk(q_ref[rows, cols], k_ref[rows, cols], v_ref[rows, cols], state_sc[h], *tabs[h])
            state_sc[h] = st
            o = o + of_ref[rows, cols]
            mu = jnp.mean(o, axis=-1, keepdims=True)
            d = o - mu
            var = jnp.mean(d * d, axis=-1, keepdims=True)
            on = d * lax.rsqrt(var + GN_EPS) * gn_w
            g = g_ref[rows, cols].astype(F32)
            y_sc[rows, CONV_CH + h * RET_DIM:CONV_CH + (h + 1) * RET_DIM] = (jax.nn.silu(g) * on).astype(BF16)

    tc = n_chunks * RET_CHUNK
    tile = nsteps - 1 - step
    u = ac_ref[...].astype(F32) * ah_ref[...].astype(F32)
    u_prev = acp_ref[SUBLANES - 1:SUBLANES, :].astype(F32) * ahp_ref[SUBLANES - 1:SUBLANES, :].astype(F32)
    u_next = acn_ref[0:1, :].astype(F32) * ahn_ref[0:1, :].astype(F32)
    u_prev = jnp.where(tile > 0, u_prev, 0.0)
    u_next = jnp.where(tile < nsteps - 1, u_next, 0.0)
    ridx = lax.broadcasted_iota(jnp.int32, u.shape, 0)
    up = jnp.where(ridx == 0, u_prev, pltpu.roll(u, 1, 0))
    un = jnp.where(ridx == tc - 1, u_next, pltpu.roll(u, tc - 1, 0))
    conv = cw_ref[0:1, :] * up + cw_ref[1:2, :] * u + cw_ref[2:3, :] * un
    y_sc[:, 0:CONV_CH] = (ab_ref[...].astype(F32) * conv).astype(BF16)

    o_ref[...] = x_ref[...] + jnp.dot(y_sc[...], wo_ref[...], preferred_element_type=F32)


def _hyb_out(x2, proj, o_f, dec, conv_w, gn_w, w_out, batch, seq, tc):
    t = x2.shape[0]
    nsteps = seq // tc
    rpt = tc // SUBLANES
    n8 = t // SUBLANES
    tile = lambda b, i: b * nsteps + (nsteps - 1 - i)

    def blk(col, width=RET_WIDTH):
        return pl.BlockSpec((tc, width), lambda b, i: (tile(b, i), col))

    def halo_prev(col):
        return pl.BlockSpec((SUBLANES, CONV_CH), lambda b, i: (jnp.maximum(tile(b, i) * rpt - 1, 0), col))

    def halo_next(col):
        return pl.BlockSpec((SUBLANES, CONV_CH), lambda b, i: (jnp.minimum((tile(b, i) + 1) * rpt, n8 - 1), col))

    const = lambda b, i: (0, 0)
    return pl.pallas_call(
        functools.partial(_hyb_out_kernel, n_chunks=tc // RET_CHUNK, nsteps=nsteps),
        grid=(batch, nsteps),
        in_specs=[blk(0, D_MODEL),
                  blk(0), blk(1), blk(2), blk(3), blk(4), blk(5), blk(6), blk(0),
                  halo_prev(1), halo_prev(2), halo_next(1), halo_next(2),
                  pl.BlockSpec((1, RET_HEADS), const),
                  pl.BlockSpec((3, CONV_CH), const),
                  pl.BlockSpec((1, RET_WIDTH), const),
                  pl.BlockSpec((CONV_CH + RET_WIDTH, D_MODEL), const)],
        out_specs=blk(0, D_MODEL),
        out_shape=jax.ShapeDtypeStruct((t, D_MODEL), F32),
        scratch_shapes=[pltpu.VMEM((RET_HEADS, RET_DIM, RET_DIM), F32),
                        pltpu.VMEM((tc, CONV_CH + RET_WIDTH), BF16)],
        compiler_params=pltpu.CompilerParams(dimension_semantics=("arbitrary", "arbitrary"),
                                             vmem_limit_bytes=VMEM_LIMIT),
        name="hyb_out",
    )(x2, proj, proj, proj, proj, proj, proj, proj, o_f, proj, proj, proj, proj, dec, conv_w, gn_w, w_out)


def _ffn_kernel(*refs, pre_proj, final_norm):
    refs = list(refs)
    x_ref = refs.pop(0)
    if pre_proj:
        a_ref, wo_ref = refs.pop(0), refs.pop(0)
    nw_ref, wg_ref, wu_ref, wd_ref = refs[:4]
    refs = refs[4:]
    if final_norm:
        fw_ref = refs.pop(0)
    o_ref = refs.pop(0)

    x = x_ref[...]
    if pre_proj:
        x = x + jnp.dot(a_ref[...], wo_ref[...], preferred_element_type=F32)
    xn = _rms(x, nw_ref[...], NORM_EPS).astype(BF16)
    acc = x
    for c in range(D_FF // FF_CHUNK):
        cs = slice(c * FF_CHUNK, (c + 1) * FF_CHUNK)
        g = jnp.dot(xn, wg_ref[:, cs], preferred_element_type=F32)
        u = jnp.dot(xn, wu_ref[:, cs], preferred_element_type=F32)
        h = (jax.nn.silu(g) * u).astype(BF16)
        acc = acc + jnp.dot(h, wd_ref[cs, :], preferred_element_type=F32)
    if final_norm:
        acc = _rms(acc, fw_ref[...], NORM_EPS)
    o_ref[...] = acc


def _ffn(x2, nw, wg, wu, wd, tm, pre=None, final_w=None):
    t = x2.shape[0]
    row = lambda i: (i, 0)
    const = lambda i: (0, 0)
    single = pl.Buffered(1)
    args = [x2]
    specs = [pl.BlockSpec((tm, D_MODEL), row)]
    if pre is not None:
        a, wo = pre
        args += [a, wo]
        specs += [pl.BlockSpec((tm, D_MODEL), row),
                  pl.BlockSpec((D_MODEL, D_MODEL), const, pipeline_mode=single)]
    args += [nw, wg, wu, wd]
    specs += [pl.BlockSpec((1, D_MODEL), const),
              pl.BlockSpec((D_MODEL, D_FF), const, pipeline_mode=single),
              pl.BlockSpec((D_MODEL, D_FF), const, pipeline_mode=single),
              pl.BlockSpec((D_FF, D_MODEL), const, pipeline_mode=single)]
    if final_w is not None:
        args.append(final_w)
        specs.append(pl.BlockSpec((1, D_MODEL), const))
    return pl.pallas_call(
        functools.partial(_ffn_kernel, pre_proj=pre is not None, final_norm=final_w is not None),
        grid=(t // tm,),
        in_specs=specs,
        out_specs=pl.BlockSpec((tm, D_MODEL), row),
        out_shape=jax.ShapeDtypeStruct((t, D_MODEL), F32),
        compiler_params=pltpu.CompilerParams(dimension_semantics=("arbitrary",),
                                             vmem_limit_bytes=VMEM_LIMIT),
        name="ffn",
    )(*args)


FP8_TOP_EXP = 7
FP8_EXP_CLIP = 60


def _pow2_scale(m):
    exp = (lax.bitcast_convert_type(m, jnp.int32) >> 23) - 127
    es = jnp.clip(FP8_TOP_EXP - exp, -FP8_EXP_CLIP, FP8_EXP_CLIP)
    scale = lax.bitcast_convert_type((es + 127) << 23, F32)
    inv = lax.bitcast_convert_type((127 - es) << 23, F32)
    return scale, inv


def _fp8_split(x):
    hi = x.astype(F8).astype(F32)
    return hi, x - hi


def _absmax(x):
    return jnp.max(jnp.max(x, axis=0, keepdims=True), axis=1, keepdims=True)


def _qkv_kernel(x_ref, nw_ref, wqt_ref, wk_ref, wvt_ref, cos_ref, slo_ref, shi_ref, cost_ref, sint_ref,
                qt_ref, kx_ref, kinv_ref, vt_ref):
    xn = _rms(x_ref[...], nw_ref[...], NORM_EPS).astype(BF16)
    xnt = xn.T
    half = DIFF_DIM // 2

    cost, sint = cost_ref[...], sint_ref[...]
    for c in range(D_MODEL // COL_CHUNK):
        y = jnp.dot(wqt_ref[c * COL_CHUNK:(c + 1) * COL_CHUNK, :], xnt, preferred_element_type=F32)
        for hrow in range(0, COL_CHUNK, DIFF_DIM):
            x1, x2 = y[hrow:hrow + half], y[hrow + half:hrow + DIFF_DIM]
            lo = c * COL_CHUNK + hrow
            qscale = DIFF_DIM ** -0.5 * LOG2E
            qt_ref[0, lo:lo + half, :] = ((x1 * cost - x2 * sint) * qscale).astype(BF16)
            qt_ref[0, lo + half:lo + DIFF_DIM, :] = ((x2 * cost + x1 * sint) * qscale).astype(BF16)

    cos, slo, shi = cos_ref[...], slo_ref[...], shi_ref[...]
    head_lane = lax.broadcasted_iota(jnp.int32, (1, 2 * DIFF_HEADS), 1)
    kinv = jnp.zeros((1, 2 * DIFF_HEADS), F32)
    for c in range(D_MODEL // COL_CHUNK):
        y = jnp.dot(xn, wk_ref[:, c * COL_CHUNK:(c + 1) * COL_CHUNK], preferred_element_type=F32)
        for g in range(COL_CHUNK // LANES):
            r = _rope128(y[:, g * LANES:(g + 1) * LANES], cos, slo, shi, half)
            first = lax.broadcasted_iota(jnp.int32, r.shape, 1) < DIFF_DIM
            colmax = jnp.max(jnp.abs(r), axis=0, keepdims=True)
            sc1, inv1 = _pow2_scale(jnp.max(jnp.where(first[0:1], colmax, 0.0), axis=1, keepdims=True))
            sc2, inv2 = _pow2_scale(jnp.max(jnp.where(first[0:1], 0.0, colmax), axis=1, keepdims=True))
            hi, lw = _fp8_split(r * jnp.where(first[0:1], sc1, sc2))
            g1 = jnp.where(first, hi, pltpu.roll(lw, DIFF_DIM, 1)).astype(F8)
            g2 = jnp.where(first, pltpu.roll(hi, DIFF_DIM, 1), lw).astype(F8)
            pair = c * (COL_CHUNK // LANES) + g
            base = pair * 4 * LANES
            kx_ref[:, base:base + LANES] = g1
            kx_ref[:, base + LANES:base + 2 * LANES] = g1
            kx_ref[:, base + 2 * LANES:base + 3 * LANES] = g2
            kx_ref[:, base + 3 * LANES:base + 4 * LANES] = g2
            kinv = jnp.where(head_lane == 2 * pair, inv1, kinv)
            kinv = jnp.where(head_lane == 2 * pair + 1, inv2, kinv)
    kinv_ref[0] = kinv

    for c in range(D_MODEL // COL_CHUNK):
        y = jnp.dot(wvt_ref[c * COL_CHUNK:(c + 1) * COL_CHUNK, :], xnt, preferred_element_type=F32)
        for r0 in range(0, COL_CHUNK, V_DIM):
            head = (c * COL_CHUNK + r0) // V_DIM
            vt_ref[0, 0, head, 0:V_DIM, :] = y[r0:r0 + V_DIM].astype(BF16)
            vt_ref[0, 0, head, V_DIM:V_DIM + ONES_ROWS, :] = jnp.ones((ONES_ROWS, x_ref.shape[0]), BF16)


def _qkv(x2, nw, wq_t, wk, wv_t, tabs, tabs_t, batch, seq, tm):
    t = x2.shape[0]
    spt = seq // tm
    row = lambda i: (i, 0)
    pos = lambda i: (i % spt, 0)
    pos_t = lambda i: (0, i % spt)
    const = lambda i: (0, 0)
    sq = (D_MODEL, D_MODEL)
    return pl.pallas_call(
        _qkv_kernel,
        grid=(t // tm,),
        in_specs=[pl.BlockSpec((tm, D_MODEL), row),
                  pl.BlockSpec((1, D_MODEL), const),
                  pl.BlockSpec(sq, const), pl.BlockSpec(sq, const), pl.BlockSpec(sq, const),
                  pl.BlockSpec((tm, LANES), pos),
                  pl.BlockSpec((tm, LANES), pos),
                  pl.BlockSpec((tm, LANES), pos),
                  pl.BlockSpec((DIFF_DIM // 2, tm), pos_t),
                  pl.BlockSpec((DIFF_DIM // 2, tm), pos_t)],
        out_specs=[pl.BlockSpec((1, D_MODEL, tm), lambda i: (i // spt, 0, i % spt)),
                   pl.BlockSpec((tm, 4 * D_MODEL), row),
                   pl.BlockSpec((1, 1, 2 * DIFF_HEADS), lambda i: (i, 0, 0)),
                   pl.BlockSpec((1, 1, DIFF_HEADS, V_DIM + ONES_ROWS, tm), lambda i: (i // spt, i % spt, 0, 0, 0))],
        out_shape=[jax.ShapeDtypeStruct((batch, D_MODEL, seq), BF16),
                   jax.ShapeDtypeStruct((t, 4 * D_MODEL), F8),
                   jax.ShapeDtypeStruct((t // tm, 1, 2 * DIFF_HEADS), F32),
                   jax.ShapeDtypeStruct((batch, spt, DIFF_HEADS, V_DIM + ONES_ROWS, tm), BF16)],
        compiler_params=pltpu.CompilerParams(dimension_semantics=("arbitrary",),
                                             vmem_limit_bytes=VMEM_LIMIT),
        name="qkv",
    )(x2, nw, wq_t, wk, wv_t, *tabs, *tabs_t)


ONES_ROWS = 16


def _attn_kernel(kinv_ref, qt_ref, kx_ref, vt_ref, lq1_ref, lk1_ref, lq2_ref, lk2_ref, sub_ref, o_ref,
                 qx_sc, s_a, s_b, mx_a, mx_b, m_sc, acc_sc, *, tk, nk, unroll, lambda_init):
    hd = 2 * DIFF_DIM
    qt = qt_ref[0].astype(F32)
    qinv = []
    for idx in range(2):
        qh = qt[idx * DIFF_DIM:(idx + 1) * DIFF_DIM]
        scale, inv = _pow2_scale(_absmax(jnp.abs(qh)))
        hi, lw = _fp8_split(qh * scale)
        qx_sc[idx] = jnp.concatenate([hi, hi, lw, lw], axis=0).astype(F8)
        qinv.append(inv)
    m_sc[...] = jnp.full_like(m_sc, -jnp.inf)
    acc_sc[...] = jnp.zeros_like(acc_sc)
    kinv_base = (pl.program_id(0) * nk) * (2 * DIFF_HEADS) + 2 * pl.program_id(1)

    subs = [(idx, slice(c0, c0 + MXU_COLS)) for idx in range(2) for c0 in range(0, qt.shape[1], MXU_COLS)]

    def scores_sub(j, s_ref, mx_ref, idx, cs):
        off = pl.multiple_of(j * tk, tk)
        kx = kx_ref[pl.ds(off, tk), idx * 4 * DIFF_DIM:(idx + 1) * 4 * DIFF_DIM]
        st = jnp.dot(kx, qx_sc[idx, :, cs], preferred_element_type=F32).astype(BF16)
        s_ref[idx, :, cs] = st
        mx_ref[idx, :, cs] = jnp.max(st, axis=0, keepdims=True).astype(F32)

    def accumulate_sub(j, s_ref, mx_ref, idx, cs):
        vt = vt_ref[0, j, 0]
        c = kinv_ref[kinv_base + j * (2 * DIFF_HEADS) + idx] * qinv[idx]
        m_prev = m_sc[idx, :, cs]
        m_new = jnp.maximum(m_prev, mx_ref[idx, :, cs] * c)
        alpha = jnp.exp2(m_prev - m_new)
        p = jnp.exp2(s_ref[idx, :, cs] * c.astype(BF16) - m_new.astype(BF16))
        acc_sc[idx, :, cs] = acc_sc[idx, :, cs] * alpha + jnp.dot(vt, p, preferred_element_type=F32)
        m_sc[idx, :, cs] = m_new

    bufs = ((s_a, mx_a), (s_b, mx_b))

    def step(j, parity):
        for idx, cs in subs:
            accumulate_sub(j, *bufs[parity], idx, cs)
            scores_sub(j + 1, *bufs[1 - parity], idx, cs)

    for idx, cs in subs:
        scores_sub(0, *bufs[0], idx, cs)

    def body(i, carry):
        for u in range(unroll):
            step(unroll * i + u, u % 2)
        return carry

    n_loop = (nk - 1) // unroll
    lax.fori_loop(0, n_loop, body, 0)
    for j in range(n_loop * unroll, nk - 1):
        step(j, j % 2)
    for idx, cs in subs:
        accumulate_sub(nk - 1, *bufs[(nk - 1) % 2], idx, cs)

    lam = (jnp.exp(jnp.sum(lq1_ref[...] * lk1_ref[...], axis=-1, keepdims=True))
           - jnp.exp(jnp.sum(lq2_ref[...] * lk2_ref[...], axis=-1, keepdims=True)) + lambda_init)
    o = (acc_sc[0, 0:hd, :] / acc_sc[0, hd:hd + 1, :]
         - lam * (acc_sc[1, 0:hd, :] / acc_sc[1, hd:hd + 1, :]))
    ms = jnp.mean(o * o, axis=0, keepdims=True)
    o = (o * lax.rsqrt(ms + SUBLN_EPS)).T
    o_ref[...] = (o * sub_ref[...] * (1.0 - lambda_init)).astype(BF16)


def _pick_unroll(nk):
    steps = nk - 1
    fits = [u for u in range(2, ATT_MAX_BLOCK_STEPS + 1, 2)
            if steps // u >= 2 and steps % u + 1 <= ATT_MAX_BLOCK_STEPS]
    return max(fits, default=2)


def _attn(qt, kx, kinv, vt, lq1, lk1, lq2, lk2, subln, batch, seq, tq, tk, lambda_init):
    nq = seq // tq
    nk = seq // tk
    hd = 2 * DIFF_DIM
    const = lambda b, h, i: (0, 0)
    return pl.pallas_call(
        functools.partial(_attn_kernel, tk=tk, nk=nk, unroll=_pick_unroll(nk), lambda_init=lambda_init),
        grid=(batch, DIFF_HEADS, nq),
        in_specs=[pl.BlockSpec(memory_space=pltpu.SMEM),
                  pl.BlockSpec((1, hd, tq), lambda b, h, i: (b, h, i)),
                  pl.BlockSpec((seq, 4 * hd), lambda b, h, i: (b, h)),
                  pl.BlockSpec((1, nk, 1, hd + ONES_ROWS, tk), lambda b, h, i: (b, 0, h, 0, 0)),
                  pl.BlockSpec((1, DIFF_DIM), const), pl.BlockSpec((1, DIFF_DIM), const),
                  pl.BlockSpec((1, DIFF_DIM), const), pl.BlockSpec((1, DIFF_DIM), const),
                  pl.BlockSpec((1, hd), const)],
        out_specs=pl.BlockSpec((tq, hd), lambda b, h, i: (b * nq + i, h)),
        out_shape=jax.ShapeDtypeStruct((batch * seq, DIFF_HEADS * hd), BF16),
        scratch_shapes=[pltpu.VMEM((2, 4 * DIFF_DIM, tq), F8),
                        pltpu.VMEM((2, tk, tq), BF16), pltpu.VMEM((2, tk, tq), BF16),
                        pltpu.VMEM((2, 1, tq), F32), pltpu.VMEM((2, 1, tq), F32),
                        pltpu.VMEM((2, 1, tq), F32),
                        pltpu.VMEM((2, hd + ONES_ROWS, tq), F32)],
        compiler_params=pltpu.CompilerParams(dimension_semantics=("arbitrary", "arbitrary", "arbitrary"),
                                             vmem_limit_bytes=VMEM_LIMIT),
        name="attn",
    )(kinv.reshape(-1), qt, kx, vt, lq1, lk1, lq2, lk2, subln)


def _trunk(x, p):
    batch, seq, _ = x.shape
    t = batch * seq
    tm = min(ROW_TILE, seq)
    tq = min(ATT_TQ, seq)
    tk = min(ATT_TK, seq)
    x2 = x.reshape(t, D_MODEL)

    proj, o_f = _hyb_in(x2, p["norm_mix"][0:1], p["hyb_w_in"], _rope_tables(seq, RET_DIM), p["hyb_decay_fwd"],
                        seq, tm)
    x2 = _hyb_out(x2, proj, o_f, p["hyb_decay_bwd"], p["hyb_conv_w"], p["hyb_gn"], p["hyb_w_out"],
                  batch, seq, tm)
    x2 = _ffn(x2, p["norm_ffn"][0:1], p["ffn_w_gate"][0], p["ffn_w_up"][0], p["ffn_w_down"][0], tm)

    lambda_init = 0.8 - 0.6 * math.exp(-0.3 * 1)
    qt, kx, kinv, vt = _qkv(x2, p["norm_mix"][1:2], p["diff_wq_t"], p["diff_wk"], p["diff_wv_t"],
                            _rope_tables(seq, DIFF_DIM), _rope_tables_t(seq, DIFF_DIM), batch, seq, tk)
    att = _attn(qt, kx, kinv, vt, p["diff_lq1"], p["diff_lk1"], p["diff_lq2"], p["diff_lk2"], p["diff_subln"],
                batch, seq, tq, tk, lambda_init)
    x2 = _ffn(x2, p["norm_ffn"][1:2], p["ffn_w_gate"][1], p["ffn_w_up"][1], p["ffn_w_down"][1], tm,
              pre=(att, p["diff_w_out"]), final_w=p["norm_final"])
    return x2.reshape(batch, seq, D_MODEL)


def kernel(x_prompt, x_sample, norm_mix, norm_ffn, norm_final, hyb_w_in, hyb_conv_w, hyb_decay_fwd,
           hyb_decay_bwd, hyb_gn, hyb_w_out, diff_w_qkv, diff_lq1, diff_lk1, diff_lq2, diff_lk2,
           diff_subln, diff_w_out, ffn_w_gate, ffn_w_up, ffn_w_down):
    assert norm_mix.shape[0] == 2 and hyb_w_in.shape[0] == 1 and diff_w_qkv.shape[0] == 1
    p = {
        "norm_mix": norm_mix, "norm_ffn": norm_ffn, "norm_final": norm_final.reshape(1, D_MODEL),
        "hyb_w_in": hyb_w_in[0].astype(BF16), "hyb_conv_w": hyb_conv_w[0],
        "hyb_decay_fwd": hyb_decay_fwd, "hyb_decay_bwd": hyb_decay_bwd,
        "hyb_gn": hyb_gn, "hyb_w_out": hyb_w_out[0].astype(BF16),
        "diff_wq_t": diff_w_qkv[0][:, :D_MODEL].T.astype(BF16),
        "diff_wk": diff_w_qkv[0][:, D_MODEL:2 * D_MODEL].astype(BF16),
        "diff_wv_t": diff_w_qkv[0][:, 2 * D_MODEL:].T.astype(BF16),
        "diff_lq1": diff_lq1, "diff_lk1": diff_lk1, "diff_lq2": diff_lq2, "diff_lk2": diff_lk2,
        "diff_subln": diff_subln, "diff_w_out": diff_w_out[0].astype(BF16),
        "ffn_w_gate": ffn_w_gate.astype(BF16), "ffn_w_up": ffn_w_up.astype(BF16),
        "ffn_w_down": ffn_w_down.astype(BF16),
    }
    return (_trunk(x_prompt, p), _trunk(x_sample, p))
```

```python
import functools
import math

import jax
import jax.numpy as jnp
from jax import lax
from jax.experimental import pallas as pl
from jax.experimental.pallas import tpu as pltpu

F32 = jnp.float32
BF16 = jnp.bfloat16
F8 = jnp.float8_e4m3fn

D_MODEL = 1024
CONV_CH = 512
RET_HEADS = 4
RET_DIM = 128
RET_WIDTH = RET_HEADS * RET_DIM
RET_CHUNK = 128
DIFF_HEADS = 8
DIFF_DIM = 64
V_DIM = 2 * DIFF_DIM
D_FF = 2816
ROPE_THETA = 10000.0
NORM_EPS = 1e-6
LOG2E = math.log2(math.e)
GN_EPS = 1e-5
SUBLN_EPS = 1e-5
IN_PROJ_COLS = 3 * CONV_CH + 4 * RET_WIDTH

LANES = 128
SUBLANES = 8
COL_CHUNK = 512
MXU_COLS = 256
FF_CHUNK = MXU_COLS
ROW_TILE = 512
ATT_TQ = 2048
ATT_TK = 1024
ATT_MAX_BLOCK_STEPS = 5
VMEM_LIMIT = 60 * 1024 * 1024


def _rms(x, w, eps):
    ms = jnp.mean(x * x, axis=-1, keepdims=True)
    return (x * lax.rsqrt(ms + eps)) * w


def _rope_tables(seq, dim):
    half = dim // 2
    inv = ROPE_THETA ** (-jnp.arange(0, dim, 2, dtype=F32) / dim)
    ang = jnp.arange(seq, dtype=F32)[:, None] * inv[None, :]
    cos = jnp.cos(ang)
    sin = jnp.sin(ang)
    zero = jnp.zeros_like(sin)
    reps = LANES // dim
    cos_t = jnp.tile(jnp.concatenate([cos, cos], axis=1), (1, reps))
    sin_lo = jnp.tile(jnp.concatenate([-sin, zero], axis=1), (1, reps))
    sin_hi = jnp.tile(jnp.concatenate([zero, sin], axis=1), (1, reps))
    return cos_t, sin_lo, sin_hi


def _rope_tables_t(seq, dim):
    inv = ROPE_THETA ** (-jnp.arange(0, dim, 2, dtype=F32) / dim)
    ang = jnp.arange(seq, dtype=F32)[:, None] * inv[None, :]
    return jnp.cos(ang).T, jnp.sin(ang).T


def _rope128(y, cos, sin_lo, sin_hi, half):
    return (y * cos + pltpu.roll(y, LANES - half, 1) * sin_lo + pltpu.roll(y, half, 1) * sin_hi)


def _hyb_in_kernel(x_ref, nw_ref, w_ref, cos_ref, slo_ref, shi_ref, dec_ref, o_ref, of_ref, state_sc, *, spt):
    @pl.when(pl.program_id(0) % spt == 0)
    def _():
        state_sc[...] = jnp.zeros_like(state_sc)

    xn = _rms(x_ref[...], nw_ref[...], NORM_EPS).astype(BF16)
    cos, slo, shi = cos_ref[...], slo_ref[...], shi_ref[...]
    q_chunk = 3 * CONV_CH // COL_CHUNK

    def project(c):
        y = jnp.dot(xn, w_ref[:, c * COL_CHUNK:(c + 1) * COL_CHUNK], preferred_element_type=F32)
        if c in (q_chunk, q_chunk + 1):
            scale = 1.0 if c == q_chunk else RET_DIM ** -0.5
            for g in range(COL_CHUNK // LANES):
                r = _rope128(y[:, g * LANES:(g + 1) * LANES], cos, slo, shi, RET_DIM // 2)
                if scale != 1.0:
                    r = r * scale
                o_ref[:, c * COL_CHUNK + g * LANES:c * COL_CHUNK + (g + 1) * LANES] = r.astype(BF16)
        else:
            o_ref[:, c * COL_CHUNK:(c + 1) * COL_CHUNK] = y.astype(BF16)

    for c in (q_chunk, q_chunk + 1, q_chunk + 2):
        project(c)

    tabs = [_decay_tables(-jnp.exp(dec_ref[:, h:h + 1]), strict=False) for h in range(RET_HEADS)]
    qb, kb, vb = (c * COL_CHUNK for c in (q_chunk, q_chunk + 1, q_chunk + 2))
    for c in range(x_ref.shape[0] // RET_CHUNK):
        rows = slice(c * RET_CHUNK, (c + 1) * RET_CHUNK)
        for h in range(RET_HEADS):
            lo = h * RET_DIM
            o, st = _ret_chunk(o_ref[rows, qb + lo:qb + lo + RET_DIM], o_ref[rows, kb + lo:kb + lo + RET_DIM],
                               o_ref[rows, vb + lo:vb + lo + RET_DIM], state_sc[h], *tabs[h])
            of_ref[rows, lo:lo + RET_DIM] = o
            state_sc[h] = st

    for c in range(IN_PROJ_COLS // COL_CHUNK):
        if c not in (q_chunk, q_chunk + 1, q_chunk + 2):
            project(c)


def _hyb_in(x2, nw, w_in, tabs, dec, seq, tm):
    t = x2.shape[0]
    spt = seq // tm
    row = lambda i: (i, 0)
    pos = lambda i: (i % spt, 0)
    const = lambda i: (0, 0)
    return pl.pallas_call(
        functools.partial(_hyb_in_kernel, spt=spt),
        grid=(t // tm,),
        in_specs=[pl.BlockSpec((tm, D_MODEL), row),
                  pl.BlockSpec((1, D_MODEL), const),
                  pl.BlockSpec((D_MODEL, IN_PROJ_COLS), const),
                  pl.BlockSpec((tm, LANES), pos),
                  pl.BlockSpec((tm, LANES), pos),
                  pl.BlockSpec((tm, LANES), pos),
                  pl.BlockSpec((1, RET_HEADS), const)],
        out_specs=[pl.BlockSpec((tm, IN_PROJ_COLS), row),
                   pl.BlockSpec((tm, RET_WIDTH), row)],
        out_shape=[jax.ShapeDtypeStruct((t, IN_PROJ_COLS), BF16),
                   jax.ShapeDtypeStruct((t, RET_WIDTH), F32)],
        scratch_shapes=[pltpu.VMEM((RET_HEADS, RET_DIM, RET_DIM), F32)],
        compiler_params=pltpu.CompilerParams(dimension_semantics=("arbitrary",),
                                             vmem_limit_bytes=VMEM_LIMIT),
        name="hyb_in",
    )(x2, nw, w_in, *tabs, dec)


def _swiglu_residual(x, nw_ref, wg_ref, wu_ref, wd_ref):
    xn = _rms(x, nw_ref[...], NORM_EPS).astype(BF16)
    acc = x
    for c in range(D_FF // FF_CHUNK):
        cs = slice(c * FF_CHUNK, (c + 1) * FF_CHUNK)
        g = jnp.dot(xn, wg_ref[:, cs], preferred_element_type=F32)
        u = jnp.dot(xn, wu_ref[:, cs], preferred_element_type=F32)
        h = (jax.nn.silu(g) * u).astype(BF16)
        acc = acc + jnp.dot(h, wd_ref[cs, :], preferred_element_type=F32)
    return acc


def _decay_tables(lg, strict):
    c = RET_CHUNK
    ii = lax.broadcasted_iota(jnp.int32, (c, c), 0).astype(F32)
    jj = lax.broadcasted_iota(jnp.int32, (c, c), 1).astype(F32)
    if strict:
        rel = jj - ii
        mask = rel > 0
        kdec = jnp.exp(lg * ii)
        qdec = jnp.exp(lg * (c - ii))
    else:
        rel = ii - jj
        mask = rel >= 0
        kdec = jnp.exp(lg * (c - 1.0 - ii))
        qdec = jnp.exp(lg * (ii + 1.0))
    dintra = jnp.where(mask, jnp.exp(lg * jnp.maximum(rel, 0.0)), 0.0)
    return dintra, kdec, qdec, jnp.exp(lg * c)


def _ret_chunk(q, k, v, st, dintra, kdec, qdec, cdec):
    sc = lax.dot_general(q, k, (((1,), (1,)), ((), ())), preferred_element_type=F32) * dintra
    lhs = jnp.concatenate([sc.astype(BF16), (q.astype(F32) * qdec).astype(BF16)], axis=1)
    o = jnp.dot(lhs, jnp.concatenate([v, st.astype(BF16)], axis=0), preferred_element_type=F32)
    kd_t = (k.astype(F32) * kdec).T.astype(BF16)
    kv = jnp.dot(kd_t, v, preferred_element_type=F32)
    return o, st * cdec + kv


def _hyb_out_kernel(x_ref, ab_ref, ac_ref, ah_ref, q_ref, k_ref, v_ref, g_ref, of_ref,
                    acp_ref, ahp_ref, acn_ref, ahn_ref, dec_ref, cw_ref, gn_ref, wo_ref,
                    nwf_ref, wg_ref, wu_ref, wd_ref, o_ref, state_sc, y_sc, *, n_chunks, nsteps):
    step = pl.program_id(1)

    @pl.when(step == 0)
    def _():
        state_sc[...] = jnp.zeros_like(state_sc)

    tabs = [_decay_tables(-jnp.exp(dec_ref[:, h:h + 1]), strict=True) for h in range(RET_HEADS)]
    for c in reversed(range(n_chunks)):
        rows = slice(c * RET_CHUNK, (c + 1) * RET_CHUNK)
        for h in range(RET_HEADS):
            cols = slice(h * RET_DIM, (h + 1) * RET_DIM)
            gn_w = gn_ref[:, cols]
            o, st = _ret_chunk(q_ref[rows, cols], k_ref[rows, cols], v_ref[rows, cols], state_sc[h], *tabs[h])
            state_sc[h] = st
            o = o + of_ref[rows, cols]
            mu = jnp.mean(o, axis=-1, keepdims=True)
            d = o - mu
            var = jnp.mean(d * d, axis=-1, keepdims=True)
            on = d * lax.rsqrt(var + GN_EPS) * gn_w
            g = g_ref[rows, cols].astype(F32)
            y_sc[rows, CONV_CH + h * RET_DIM:CONV_CH + (h + 1) * RET_DIM] = (jax.nn.silu(g) * on).astype(BF16)

    tc = n_chunks * RET_CHUNK
    tile = nsteps - 1 - step
    u = ac_ref[...].astype(F32) * ah_ref[...].astype(F32)
    u_prev = acp_ref[SUBLANES - 1:SUBLANES, :].astype(F32) * ahp_ref[SUBLANES - 1:SUBLANES, :].astype(F32)
    u_next = acn_ref[0:1, :].astype(F32) * ahn_ref[0:1, :].astype(F32)
    u_prev = jnp.where(tile > 0, u_prev, 0.0)
    u_next = jnp.where(tile < nsteps - 1, u_next, 0.0)
    ridx = lax.broadcasted_iota(jnp.int32, u.shape, 0)
    up = jnp.where(ridx == 0, u_prev, pltpu.roll(u, 1, 0))
    un = jnp.where(ridx == tc - 1, u_next, pltpu.roll(u, tc - 1, 0))
    conv = cw_ref[0:1, :] * up + cw_ref[1:2, :] * u + cw_ref[2:3, :] * un
    y_sc[:, 0:CONV_CH] = (ab_ref[...].astype(F32) * conv).astype(BF16)

    x1 = x_ref[...] + jnp.dot(y_sc[...], wo_ref[...], preferred_element_type=F32)
    o_ref[...] = _swiglu_residual(x1, nwf_ref, wg_ref, wu_ref, wd_ref)


def _hyb_out(x2, proj, o_f, dec, conv_w, gn_w, w_out, nw_ffn, wg, wu, wd, batch, seq, tc):
    t = x2.shape[0]
    nsteps = seq // tc
    rpt = tc // SUBLANES
    n8 = t // SUBLANES
    tile = lambda b, i: b * nsteps + (nsteps - 1 - i)

    def blk(col, width=RET_WIDTH):
        return pl.BlockSpec((tc, width), lambda b, i: (tile(b, i), col))

    def halo_prev(col):
        return pl.BlockSpec((SUBLANES, CONV_CH), lambda b, i: (jnp.maximum(tile(b, i) * rpt - 1, 0), col))

    def halo_next(col):
        return pl.BlockSpec((SUBLANES, CONV_CH), lambda b, i: (jnp.minimum((tile(b, i) + 1) * rpt, n8 - 1), col))

    const = lambda b, i: (0, 0)
    return pl.pallas_call(
        functools.partial(_hyb_out_kernel, n_chunks=tc // RET_CHUNK, nsteps=nsteps),
        grid=(batch, nsteps),
        in_specs=[blk(0, D_MODEL),
                  blk(0), blk(1), blk(2), blk(3), blk(4), blk(5), blk(6), blk(0),
                  halo_prev(1), halo_prev(2), halo_next(1), halo_next(2),
                  pl.BlockSpec((1, RET_HEADS), const),
                  pl.BlockSpec((3, CONV_CH), const),
                  pl.BlockSpec((1, RET_WIDTH), const),
                  pl.BlockSpec((CONV_CH + RET_WIDTH, D_MODEL), const),
                  pl.BlockSpec((1, D_MODEL), const),
                  pl.BlockSpec((D_MODEL, D_FF), const, pipeline_mode=pl.Buffered(1)),
                  pl.BlockSpec((D_MODEL, D_FF), const, pipeline_mode=pl.Buffered(1)),
                  pl.BlockSpec((D_FF, D_MODEL), const, pipeline_mode=pl.Buffered(1))],
        out_specs=blk(0, D_MODEL),
        out_shape=jax.ShapeDtypeStruct((t, D_MODEL), F32),
        scratch_shapes=[pltpu.VMEM((RET_HEADS, RET_DIM, RET_DIM), F32),
                        pltpu.VMEM((tc, CONV_CH + RET_WIDTH), BF16)],
        compiler_params=pltpu.CompilerParams(dimension_semantics=("arbitrary", "arbitrary"),
                                             vmem_limit_bytes=VMEM_LIMIT),
        name="hyb_out",
    )(x2, proj, proj, proj, proj, proj, proj, proj, o_f, proj, proj, proj, proj, dec, conv_w, gn_w, w_out,
      nw_ffn, wg, wu, wd)


def _ffn_kernel(*refs, pre_proj, final_norm):
    refs = list(refs)
    x_ref = refs.pop(0)
    if pre_proj:
        a_ref, wo_ref = refs.pop(0), refs.pop(0)
    nw_ref, wg_ref, wu_ref, wd_ref = refs[:4]
    refs = refs[4:]
    if final_norm:
        fw_ref = refs.pop(0)
    o_ref = refs.pop(0)

    x = x_ref[...]
    if pre_proj:
        x = x + jnp.dot(a_ref[...], wo_ref[...], preferred_element_type=F32)
    acc = _swiglu_residual(x, nw_ref, wg_ref, wu_ref, wd_ref)
    if final_norm:
        acc = _rms(acc, fw_ref[...], NORM_EPS)
    o_ref[...] = acc


def _ffn(x2, nw, wg, wu, wd, tm, pre=None, final_w=None):
    t = x2.shape[0]
    row = lambda i: (i, 0)
    const = lambda i: (0, 0)
    single = pl.Buffered(1)
    args = [x2]
    specs = [pl.BlockSpec((tm, D_MODEL), row)]
    if pre is not None:
        a, wo = pre
        args += [a, wo]
        specs += [pl.BlockSpec((tm, D_MODEL), row),
                  pl.BlockSpec((D_MODEL, D_MODEL), const, pipeline_mode=single)]
    args += [nw, wg, wu, wd]
    specs += [pl.BlockSpec((1, D_MODEL), const),
              pl.BlockSpec((D_MODEL, D_FF), const, pipeline_mode=single),
              pl.BlockSpec((D_MODEL, D_FF), const, pipeline_mode=single),
              pl.BlockSpec((D_FF, D_MODEL), const, pipeline_mode=single)]
    if final_w is not None:
        args.append(final_w)
        specs.append(pl.BlockSpec((1, D_MODEL), const))
    return pl.pallas_call(
        functools.partial(_ffn_kernel, pre_proj=pre is not None, final_norm=final_w is not None),
        grid=(t // tm,),
        in_specs=specs,
        out_specs=pl.BlockSpec((tm, D_MODEL), row),
        out_shape=jax.ShapeDtypeStruct((t, D_MODEL), F32),
        compiler_params=pltpu.CompilerParams(dimension_semantics=("arbitrary",),
                                             vmem_limit_bytes=VMEM_LIMIT),
        name="ffn",
    )(*args)


FP8_TOP_EXP = 7
FP8_EXP_CLIP = 60


def _pow2_scale(m):
    exp = (lax.bitcast_convert_type(m, jnp.int32) >> 23) - 127
    es = jnp.clip(FP8_TOP_EXP - exp, -FP8_EXP_CLIP, FP8_EXP_CLIP)
    scale = lax.bitcast_convert_type((es + 127) << 23, F32)
    inv = lax.bitcast_convert_type((127 - es) << 23, F32)
    return scale, inv


def _fp8_split(x):
    hi = x.astype(F8).astype(F32)
    return hi, x - hi


def _absmax(x):
    return jnp.max(jnp.max(x, axis=0, keepdims=True), axis=1, keepdims=True)


def _qkv_kernel(x_ref, nw_ref, wqt_ref, wk_ref, wvt_ref, cos_ref, slo_ref, shi_ref, cost_ref, sint_ref,
                qt_ref, kx_ref, kinv_ref, vt_ref):
    xn = _rms(x_ref[...], nw_ref[...], NORM_EPS).astype(BF16)
    xnt = xn.T
    half = DIFF_DIM // 2

    cost, sint = cost_ref[...], sint_ref[...]
    for c in range(D_MODEL // COL_CHUNK):
        y = jnp.dot(wqt_ref[c * COL_CHUNK:(c + 1) * COL_CHUNK, :], xnt, preferred_element_type=F32)
        for hrow in range(0, COL_CHUNK, DIFF_DIM):
            x1, x2 = y[hrow:hrow + half], y[hrow + half:hrow + DIFF_DIM]
            lo = c * COL_CHUNK + hrow
            qscale = DIFF_DIM ** -0.5 * LOG2E
            qt_ref[0, lo:lo + half, :] = ((x1 * cost - x2 * sint) * qscale).astype(BF16)
            qt_ref[0, lo + half:lo + DIFF_DIM, :] = ((x2 * cost + x1 * sint) * qscale).astype(BF16)

    cos, slo, shi = cos_ref[...], slo_ref[...], shi_ref[...]
    head_lane = lax.broadcasted_iota(jnp.int32, (1, 2 * DIFF_HEADS), 1)
    kinv = jnp.zeros((1, 2 * DIFF_HEADS), F32)
    for c in range(D_MODEL // COL_CHUNK):
        y = jnp.dot(xn, wk_ref[:, c * COL_CHUNK:(c + 1) * COL_CHUNK], preferred_element_type=F32)
        for g in range(COL_CHUNK // LANES):
            r = _rope128(y[:, g * LANES:(g + 1) * LANES], cos, slo, shi, half)
            first = lax.broadcasted_iota(jnp.int32, r.shape, 1) < DIFF_DIM
            colmax = jnp.max(jnp.abs(r), axis=0, keepdims=True)
            sc1, inv1 = _pow2_scale(jnp.max(jnp.where(first[0:1], colmax, 0.0), axis=1, keepdims=True))
            sc2, inv2 = _pow2_scale(jnp.max(jnp.where(first[0:1], 0.0, colmax), axis=1, keepdims=True))
            hi, lw = _fp8_split(r * jnp.where(first[0:1], sc1, sc2))
            g1 = jnp.where(first, hi, pltpu.roll(lw, DIFF_DIM, 1)).astype(F8)
            g2 = jnp.where(first, pltpu.roll(hi, DIFF_DIM, 1), lw).astype(F8)
            pair = c * (COL_CHUNK // LANES) + g
            base = pair * 4 * LANES
            kx_ref[:, base:base + LANES] = g1
            kx_ref[:, base + LANES:base + 2 * LANES] = g1
            kx_ref[:, base + 2 * LANES:base + 3 * LANES] = g2
            kx_ref[:, base + 3 * LANES:base + 4 * LANES] = g2
            kinv = jnp.where(head_lane == 2 * pair, inv1, kinv)
            kinv = jnp.where(head_lane == 2 * pair + 1, inv2, kinv)
    kinv_ref[0] = kinv

    for c in range(D_MODEL // COL_CHUNK):
        y = jnp.dot(wvt_ref[c * COL_CHUNK:(c + 1) * COL_CHUNK, :], xnt, preferred_element_type=F32)
        for r0 in range(0, COL_CHUNK, V_DIM):
            head = (c * COL_CHUNK + r0) // V_DIM
            vt_ref[0, 0, head, 0:V_DIM, :] = y[r0:r0 + V_DIM].astype(BF16)
            vt_ref[0, 0, head, V_DIM:V_DIM + ONES_ROWS, :] = jnp.ones((ONES_ROWS, x_ref.shape[0]), BF16)


def _qkv(x2, nw, wq_t, wk, wv_t, tabs, tabs_t, batch, seq, tm):
    t = x2.shape[0]
    spt = seq // tm
    row = lambda i: (i, 0)
    pos = lambda i: (i % spt, 0)
    pos_t = lambda i: (0, i % spt)
    const = lambda i: (0, 0)
    sq = (D_MODEL, D_MODEL)
    return pl.pallas_call(
        _qkv_kernel,
        grid=(t // tm,),
        in_specs=[pl.BlockSpec((tm, D_MODEL), row),
                  pl.BlockSpec((1, D_MODEL), const),
                  pl.BlockSpec(sq, const), pl.BlockSpec(sq, const), pl.BlockSpec(sq, const),
                  pl.BlockSpec((tm, LANES), pos),
                  pl.BlockSpec((tm, LANES), pos),
                  pl.BlockSpec((tm, LANES), pos),
                  pl.BlockSpec((DIFF_DIM // 2, tm), pos_t),
                  pl.BlockSpec((DIFF_DIM // 2, tm), pos_t)],
        out_specs=[pl.BlockSpec((1, D_MODEL, tm), lambda i: (i // spt, 0, i % spt)),
                   pl.BlockSpec((tm, 4 * D_MODEL), row),
                   pl.BlockSpec((1, 1, 2 * DIFF_HEADS), lambda i: (i, 0, 0)),
                   pl.BlockSpec((1, 1, DIFF_HEADS, V_DIM + ONES_ROWS, tm), lambda i: (i // spt, i % spt, 0, 0, 0))],
        out_shape=[jax.ShapeDtypeStruct((batch, D_MODEL, seq), BF16),
                   jax.ShapeDtypeStruct((t, 4 * D_MODEL), F8),
                   jax.ShapeDtypeStruct((t // tm, 1, 2 * DIFF_HEADS), F32),
                   jax.ShapeDtypeStruct((batch, spt, DIFF_HEADS, V_DIM + ONES_ROWS, tm), BF16)],
        compiler_params=pltpu.CompilerParams(dimension_semantics=("arbitrary",),
                                             vmem_limit_bytes=VMEM_LIMIT),
        name="qkv",
    )(x2, nw, wq_t, wk, wv_t, *tabs, *tabs_t)


ONES_ROWS = 16


def _attn_kernel(kinv_ref, qt_ref, kx_ref, vt_ref, lq1_ref, lk1_ref, lq2_ref, lk2_ref, sub_ref, o_ref,
                 qx_sc, s_a, s_b, mx_a, mx_b, m_sc, acc_sc, *, tk, nk, unroll, lambda_init):
    hd = 2 * DIFF_DIM
    qt = qt_ref[0].astype(F32)
    qinv = []
    for idx in range(2):
        qh = qt[idx * DIFF_DIM:(idx + 1) * DIFF_DIM]
        scale, inv = _pow2_scale(_absmax(jnp.abs(qh)))
        hi, lw = _fp8_split(qh * scale)
        qx_sc[idx] = jnp.concatenate([hi, hi, lw, lw], axis=0).astype(F8)
        qinv.append(inv)
    m_sc[...] = jnp.full_like(m_sc, -jnp.inf)
    acc_sc[...] = jnp.zeros_like(acc_sc)
    kinv_base = (pl.program_id(0) * nk) * (2 * DIFF_HEADS) + 2 * pl.program_id(1)

    subs = [(idx, slice(c0, c0 + MXU_COLS)) for idx in range(2) for c0 in range(0, qt.shape[1], MXU_COLS)]

    def scores_sub(j, s_ref, mx_ref, idx, cs):
        off = pl.multiple_of(j * tk, tk)
        kx = kx_ref[pl.ds(off, tk), idx * 4 * DIFF_DIM:(idx + 1) * 4 * DIFF_DIM]
        st = jnp.dot(kx, qx_sc[idx, :, cs], preferred_element_type=F32).astype(BF16)
        s_ref[idx, :, cs] = st
        mx_ref[idx, :, cs] = jnp.max(st, axis=0, keepdims=True).astype(F32)

    def accumulate_sub(j, s_ref, mx_ref, idx, cs):
        vt = vt_ref[0, j, 0]
        c = kinv_ref[kinv_base + j * (2 * DIFF_HEADS) + idx] * qinv[idx]
        m_prev = m_sc[idx, :, cs]
        m_new = jnp.maximum(m_prev, mx_ref[idx, :, cs] * c)
        alpha = jnp.exp2(m_prev - m_new)
        p = jnp.exp2(s_ref[idx, :, cs] * c.astype(BF16) - m_new.astype(BF16))
        acc_sc[idx, :, cs] = acc_sc[idx, :, cs] * alpha + jnp.dot(vt, p, preferred_element_type=F32)
        m_sc[idx, :, cs] = m_new

    bufs = ((s_a, mx_a), (s_b, mx_b))

    def step(j, parity):
        for idx, cs in subs:
            accumulate_sub(j, *bufs[parity], idx, cs)
            scores_sub(j + 1, *bufs[1 - parity], idx, cs)

    for idx, cs in subs:
        scores_sub(0, *bufs[0], idx, cs)

    def body(i, carry):
        for u in range(unroll):
            step(unroll * i + u, u % 2)
        return carry

    n_loop = (nk - 1) // unroll
    lax.fori_loop(0, n_loop, body, 0)
    for j in range(n_loop * unroll, nk - 1):
        step(j, j % 2)
    for idx, cs in subs:
        accumulate_sub(nk - 1, *bufs[(nk - 1) % 2], idx, cs)

    lam = (jnp.exp(jnp.sum(lq1_ref[...] * lk1_ref[...], axis=-1, keepdims=True))
           - jnp.exp(jnp.sum(lq2_ref[...] * lk2_ref[...], axis=-1, keepdims=True)) + lambda_init)
    o = (acc_sc[0, 0:hd, :] / acc_sc[0, hd:hd + 1, :]
         - lam * (acc_sc[1, 0:hd, :] / acc_sc[1, hd:hd + 1, :]))
    ms = jnp.mean(o * o, axis=0, keepdims=True)
    o = (o * lax.rsqrt(ms + SUBLN_EPS)).T
    o_ref[...] = (o * sub_ref[...] * (1.0 - lambda_init)).astype(BF16)


def _pick_unroll(nk):
    steps = nk - 1
    fits = [u for u in range(2, ATT_MAX_BLOCK_STEPS + 1, 2)
            if steps // u >= 2 and steps % u + 1 <= ATT_MAX_BLOCK_STEPS]
    return max(fits, default=2)


def _attn(qt, kx, kinv, vt, lq1, lk1, lq2, lk2, subln, batch, seq, tq, tk, lambda_init):
    nq = seq // tq
    nk = seq // tk
    hd = 2 * DIFF_DIM
    const = lambda b, h, i: (0, 0)
    return pl.pallas_call(
        functools.partial(_attn_kernel, tk=tk, nk=nk, unroll=_pick_unroll(nk), lambda_init=lambda_init),
        grid=(batch, DIFF_HEADS, nq),
        in_specs=[pl.BlockSpec(memory_space=pltpu.SMEM),
                  pl.BlockSpec((1, hd, tq), lambda b, h, i: (b, h, i)),
                  pl.BlockSpec((seq, 4 * hd), lambda b, h, i: (b, h)),
                  pl.BlockSpec((1, nk, 1, hd + ONES_ROWS, tk), lambda b, h, i: (b, 0, h, 0, 0)),
                  pl.BlockSpec((1, DIFF_DIM), const), pl.BlockSpec((1, DIFF_DIM), const),
                  pl.BlockSpec((1, DIFF_DIM), const), pl.BlockSpec((1, DIFF_DIM), const),
                  pl.BlockSpec((1, hd), const)],
        out_specs=pl.BlockSpec((tq, hd), lambda b, h, i: (b * nq + i, h)),
        out_shape=jax.ShapeDtypeStruct((batch * seq, DIFF_HEADS * hd), BF16),
        scratch_shapes=[pltpu.VMEM((2, 4 * DIFF_DIM, tq), F8),
                        pltpu.VMEM((2, tk, tq), BF16), pltpu.VMEM((2, tk, tq), BF16),
                        pltpu.VMEM((2, 1, tq), F32), pltpu.VMEM((2, 1, tq), F32),
                        pltpu.VMEM((2, 1, tq), F32),
                        pltpu.VMEM((2, hd + ONES_ROWS, tq), F32)],
        compiler_params=pltpu.CompilerParams(dimension_semantics=("arbitrary", "arbitrary", "arbitrary"),
                                             vmem_limit_bytes=VMEM_LIMIT),
        name="attn",
    )(kinv.reshape(-1), qt, kx, vt, lq1, lk1, lq2, lk2, subln)


def _trunk(x, p):
    batch, seq, _ = x.shape
    t = batch * seq
    tm = min(ROW_TILE, seq)
    tq = min(ATT_TQ, seq)
    tk = min(ATT_TK, seq)
    x2 = x.reshape(t, D_MODEL)

    proj, o_f = _hyb_in(x2, p["norm_mix"][0:1], p["hyb_w_in"], _rope_tables(seq, RET_DIM), p["hyb_decay_fwd"],
                        seq, tm)
    x2 = _hyb_out(x2, proj, o_f, p["hyb_decay_bwd"], p["hyb_conv_w"], p["hyb_gn"], p["hyb_w_out"],
                  p["norm_ffn"][0:1], p["ffn_w_gate"][0], p["ffn_w_up"][0], p["ffn_w_down"][0], batch, seq, tm)

    lambda_init = 0.8 - 0.6 * math.exp(-0.3 * 1)
    qt, kx, kinv, vt = _qkv(x2, p["norm_mix"][1:2], p["diff_wq_t"], p["diff_wk"], p["diff_wv_t"],
                            _rope_tables(seq, DIFF_DIM), _rope_tables_t(seq, DIFF_DIM), batch, seq, tk)
    att = _attn(qt, kx, kinv, vt, p["diff_lq1"], p["diff_lk1"], p["diff_lq2"], p["diff_lk2"], p["diff_subln"],
                batch, seq, tq, tk, lambda_init)
    x2 = _ffn(x2, p["norm_ffn"][1:2], p["ffn_w_gate"][1], p["ffn_w_up"][1], p["ffn_w_down"][1], tm,
              pre=(att, p["diff_w_out"]), final_w=p["norm_final"])
    return x2.reshape(batch, seq, D_MODEL)


def kernel(x_prompt, x_sample, norm_mix, norm_ffn, norm_final, hyb_w_in, hyb_conv_w, hyb_decay_fwd,
           hyb_decay_bwd, hyb_gn, hyb_w_out, diff_w_qkv, diff_lq1, diff_lk1, diff_lq2, diff_lk2,
           diff_subln, diff_w_out, ffn_w_gate, ffn_w_up, ffn_w_down):
    assert norm_mix.shape[0] == 2 and hyb_w_in.shape[0] == 1 and diff_w_qkv.shape[0] == 1
    p = {
        "norm_mix": norm_mix, "norm_ffn": norm_ffn, "norm_final": norm_final.reshape(1, D_MODEL),
        "hyb_w_in": hyb_w_in[0].astype(BF16), "hyb_conv_w": hyb_conv_w[0],
        "hyb_decay_fwd": hyb_decay_fwd, "hyb_decay_bwd": hyb_decay_bwd,
        "hyb_gn": hyb_gn, "hyb_w_out": hyb_w_out[0].astype(BF16),
        "diff_wq_t": diff_w_qkv[0][:, :D_MODEL].T.astype(BF16),
        "diff_wk": diff_w_qkv[0][:, D_MODEL:2 * D_MODEL].astype(BF16),
        "diff_wv_t": diff_w_qkv[0][:, 2 * D_MODEL:].T.astype(BF16),
        "diff_lq1": diff_lq1, "diff_lk1": diff_lk1, "diff_lq2": diff_lq2, "diff_lk2": diff_lk2,
        "diff_subln": diff_subln, "diff_w_out": diff_w_out[0].astype(BF16),
        "ffn_w_gate": ffn_w_gate.astype(BF16), "ffn_w_up": ffn_w_up.astype(BF16),
        "ffn_w_down": ffn_w_down.astype(BF16),
    }
    return (_trunk(x_prompt, p), _trunk(x_sample, p))
```
